```python
import jax, jax.numpy as jnp
from jax import lax
import numpy as np

D_MODEL = 2048
BATCH = 2
SEQ = 8192
DEPTH = 2

HEAD_DIM = 128
ROT_DIM = HEAD_DIM // 4
ROPE_THETA = 500000.0
NORM_EPS = 1e-6
D_FF = 4 * D_MODEL
D_PLE = 256
N_A_LAYERS = DEPTH // 2
N_B_LAYERS = DEPTH - N_A_LAYERS
QBLOCK = 128
NEG_INF = -1e30
FORCE_SCORE = 1e9
TINY = 1e-20

DILATED_GROUPS = ((128, 1), (512, 4), (2048, 16))
HEADS_PER_GROUP_A = D_MODEL // (2 * HEAD_DIM)

NSA_HEADS = D_MODEL // HEAD_DIM
NSA_KV_GROUPS = 4
CMP_LEN = 32
CMP_STRIDE = 16
CMP_HIDDEN = 2 * HEAD_DIM
SLC_LEN = 64
SLC_TOP_N = 16
WIN_LEN = 512
N_NSA_BRANCH = 3

kernel_name = "yoco_dilated_nsa_hybrid"


def rmsnorm(x, g):
    xf = x.astype(jnp.float32)
    y = xf * lax.rsqrt(jnp.mean(xf * xf, axis=-1, keepdims=True) + NORM_EPS)
    return (y * g.astype(jnp.float32)).astype(x.dtype)


def rope_partial(x, pos):
    half = ROT_DIM // 2
    inv = ROPE_THETA ** (-jnp.arange(half, dtype=jnp.float32) * (2.0 / ROT_DIM))
    ang = pos.astype(jnp.float32)[:, None] * inv[None, :]
    cos = jnp.cos(ang)[:, None, :]
    sin = jnp.sin(ang)[:, None, :]
    xf = x.astype(jnp.float32)
    x1 = xf[..., :half]
    x2 = xf[..., half:ROT_DIM]
    out = jnp.concatenate([x1 * cos - x2 * sin, x2 * cos + x1 * sin, xf[..., ROT_DIM:]], axis=-1)
    return out.astype(x.dtype)


def dilated_mixture_attention(xn, w_in, w_out):
    B, S, _ = xn.shape
    G, Hg, dh = len(DILATED_GROUPS), HEADS_PER_GROUP_A, HEAD_DIM
    nb = S // QBLOCK
    pos = jnp.arange(S)
    qkv = (xn @ w_in).reshape(B, S, 3, G * Hg, dh)
    q = (rope_partial(qkv[:, :, 0], pos) * (dh ** -0.5)).reshape(B, S, G, Hg, dh)
    k = rope_partial(qkv[:, :, 1], pos).reshape(B, S, G, Hg, dh)
    v = qkv[:, :, 2].reshape(B, S, G, Hg, dh)
    q_blocks = q.reshape(B, nb, QBLOCK, G, Hg, dh).transpose(1, 0, 2, 3, 4, 5)

    def block(args):
        bi, qb = args
        t = bi * QBLOCK + jnp.arange(QBLOCK)
        outs, lses = [], []
        for g, (window, dil) in enumerate(DILATED_GROUPS):
            n_keys = window // dil + 1
            idx = t[:, None] - dil * jnp.arange(n_keys)[None, :]
            valid = idx >= 0
            idx = jnp.maximum(idx, 0)
            kg = jnp.take(k[:, :, g], idx, axis=1)
            vg = jnp.take(v[:, :, g], idx, axis=1)
            s = jnp.einsum('bqhd,bqnhd->bhqn', qb[:, :, g], kg, preferred_element_type=jnp.float32)
            s = jnp.where(valid[None, None], s, NEG_INF)
            lse = jax.nn.logsumexp(s, axis=-1)
            pr = jnp.exp(s - lse[..., None])
            outs.append(jnp.einsum('bhqn,bqnhd->bqhd', pr.astype(vg.dtype), vg).astype(jnp.float32))
            lses.append(lse)
        wts = jax.nn.softmax(jnp.stack(lses), axis=0)
        wts = wts.transpose(0, 1, 3, 2)[..., None]
        o = jnp.sum(wts * jnp.stack(outs), axis=0)
        return o.reshape(B, QBLOCK, Hg * dh).astype(xn.dtype)

    o = lax.map(block, (jnp.arange(nb), q_blocks))
    o = o.transpose(1, 0, 2, 3).reshape(B, S, Hg * dh)
    return o @ w_out


def compress_blocks(blocks, pe, w1, w2):
    B, n, L, G, dh = blocks.shape
    z = (blocks + pe[None, None, :, None, :]).transpose(0, 1, 3, 2, 4).reshape(B, n, G, L * dh)
    return jax.nn.gelu(z @ w1) @ w2


def nsa_shared_kv(hn, w_kv, cmp_pe_k, cmp_w1_k, cmp_w2_k, cmp_pe_v, cmp_w1_v, cmp_w2_v):
    B, S, _ = hn.shape
    G, dh = NSA_KV_GROUPS, HEAD_DIM
    pos = jnp.arange(S)
    kv = (hn @ w_kv).reshape(B, S, 6, G, dh)
    k_cmp_raw, v_cmp_raw = kv[:, :, 0], kv[:, :, 1]
    k_slc = rope_partial(kv[:, :, 2], pos)
    v_slc = kv[:, :, 3]
    k_win = rope_partial(kv[:, :, 4], pos)
    v_win = kv[:, :, 5]
    n_cmp = (S - CMP_LEN) // CMP_STRIDE + 1
    starts = jnp.arange(n_cmp) * CMP_STRIDE
    idx = starts[:, None] + jnp.arange(CMP_LEN)[None, :]
    k_cmp = compress_blocks(jnp.take(k_cmp_raw, idx, axis=1), cmp_pe_k, cmp_w1_k, cmp_w2_k)
    k_cmp = rope_partial(k_cmp, starts + CMP_LEN - 1)
    v_cmp = compress_blocks(jnp.take(v_cmp_raw, idx, axis=1), cmp_pe_v, cmp_w1_v, cmp_w2_v)
    n_slc = S // SLC_LEN
    k_slc_b = k_slc.reshape(B, n_slc, SLC_LEN, G, dh).transpose(0, 3, 1, 2, 4)
    v_slc_b = v_slc.reshape(B, n_slc, SLC_LEN, G, dh).transpose(0, 3, 1, 2, 4)
    pad = ((0, 0), (WIN_LEN, 0), (0, 0), (0, 0))
    return (k_cmp, v_cmp, k_slc_b, v_slc_b, jnp.pad(k_win, pad), jnp.pad(v_win, pad))


def nsa_attention(xn, w_qg, w_out, k_cmp, v_cmp, k_slc_b, v_slc_b, k_win_pad, v_win_pad):
    B, S, _ = xn.shape
    H, G, dh = NSA_HEADS, NSA_KV_GROUPS, HEAD_DIM
    hpg = H // G
    nb = S // QBLOCK
    n_cmp = k_cmp.shape[1]
    n_slc = k_slc_b.shape[2]
    top_n = min(SLC_TOP_N, n_slc)
    pos = jnp.arange(S)
    qg = xn @ w_qg
    q = (rope_partial(qg[..., :H * dh].reshape(B, S, H, dh), pos) * (dh ** -0.5)).reshape(B, S, G, hpg, dh)
    gates = jax.nn.sigmoid(qg[..., H * dh:].astype(jnp.float32)).reshape(B, S, N_NSA_BRANCH, G, hpg)
    q_blocks = q.reshape(B, nb, QBLOCK, G, hpg, dh).transpose(1, 0, 2, 3, 4, 5)
    g_blocks = gates.reshape(B, nb, QBLOCK, N_NSA_BRANCH, G, hpg).transpose(1, 0, 2, 3, 4, 5)

    cmp_start = jnp.arange(n_cmp) * CMP_STRIDE
    cmp_end = cmp_start + CMP_LEN - 1
    slc_start = jnp.arange(n_slc) * SLC_LEN
    overlap = jnp.clip(jnp.minimum(cmp_start[:, None] + CMP_LEN, slc_start[None, :] + SLC_LEN)
                       - jnp.maximum(cmp_start[:, None], slc_start[None, :]), 0, None)
    overlap = overlap.astype(jnp.float32) / CMP_LEN
    bix = jnp.arange(B)[:, None, None, None]
    gix = jnp.arange(G)[None, None, :, None]

    def block(args):
        bi, qb, gb = args
        t = bi * QBLOCK + jnp.arange(QBLOCK)
        s_c = jnp.einsum('bqghd,bcgd->bqghc', qb, k_cmp, preferred_element_type=jnp.float32)
        vc = (cmp_end[None, :] <= t[:, None])[None, :, None, None, :]
        s_c = jnp.where(vc, s_c, NEG_INF)
        e = jnp.where(vc, jnp.exp(s_c - jnp.max(s_c, axis=-1, keepdims=True)), 0.0)
        p_c = e / jnp.maximum(jnp.sum(e, axis=-1, keepdims=True), TINY)
        o_cmp = jnp.einsum('bqghc,bcgd->bqghd', p_c.astype(v_cmp.dtype), v_cmp).astype(jnp.float32)
        imp = jnp.einsum('bqghc,cj->bqgj', p_c, overlap)
        jidx = jnp.arange(n_slc)[None, :]
        cur = (t // SLC_LEN)[:, None]
        forced = ((jidx == 0) | (jidx == cur) | (jidx == cur - 1))[None, :, None, :]
        causal = (jidx * SLC_LEN <= t[:, None])[None, :, None, :]
        score = jnp.where(forced, FORCE_SCORE, jnp.where(causal, imp, NEG_INF))
        _, sel = lax.top_k(score, top_n)
        ks = k_slc_b[bix, gix, sel]
        vs = v_slc_b[bix, gix, sel].reshape(B, QBLOCK, G, top_n * SLC_LEN, dh)
        tok = sel[..., None] * SLC_LEN + jnp.arange(SLC_LEN)
        vsm = (tok <= t[None, :, None, None, None]).reshape(B, QBLOCK, G, 1, top_n * SLC_LEN)
        s_s = jnp.einsum('bqghd,bqgnld->bqghnl', qb, ks, preferred_element_type=jnp.float32)
        s_s = jnp.where(vsm, s_s.reshape(B, QBLOCK, G, hpg, top_n * SLC_LEN), NEG_INF)
        p_s = jax.nn.softmax(s_s, axis=-1)
        o_slc = jnp.einsum('bqghk,bqgkd->bqghd', p_s.astype(vs.dtype), vs).astype(jnp.float32)
        t0 = bi * QBLOCK
        kw = lax.dynamic_slice_in_dim(k_win_pad, t0, WIN_LEN + QBLOCK, axis=1)
        vw = lax.dynamic_slice_in_dim(v_win_pad, t0, WIN_LEN + QBLOCK, axis=1)
        kp = t0 - WIN_LEN + jnp.arange(WIN_LEN + QBLOCK)
        dist = t[:, None] - kp[None, :]
        vwm = ((dist >= 0) & (dist < WIN_LEN) & (kp[None, :] >= 0))[None, :, None, None, :]
        s_w = jnp.einsum('bqghd,bkgd->bqghk', qb, kw, preferred_element_type=jnp.float32)
        p_w = jax.nn.softmax(jnp.where(vwm, s_w, NEG_INF), axis=-1)
        o_win = jnp.einsum('bqghk,bkgd->bqghd', p_w.astype(vw.dtype), vw).astype(jnp.float32)
        o = (gb[:, :, 0, ..., None] * o_cmp + gb[:, :, 1, ..., None] * o_slc
             + gb[:, :, 2, ..., None] * o_win)
        return o.reshape(B, QBLOCK, H * dh).astype(xn.dtype)

    o = lax.map(block, (jnp.arange(nb), q_blocks, g_blocks))
    o = o.transpose(1, 0, 2, 3).reshape(B, S, H * dh)
    return o @ w_out


def setup_inputs(seed: int = 0) -> dict:
    key = jax.random.key(seed)
    ks = jax.random.split(key, 24)
    f32 = jnp.float32

    def nrm(k, shape, fan_in, scale=1.0):
        return jax.random.normal(k, shape, f32) * (scale * fan_in ** -0.5)

    def gain(k, shape):
        return 1.0 + 0.02 * jax.random.normal(k, shape, f32)

    GA = len(DILATED_GROUPS)
    a_in_cols = 3 * GA * HEADS_PER_GROUP_A * HEAD_DIM
    a_out_rows = HEADS_PER_GROUP_A * HEAD_DIM
    b_q_cols = NSA_HEADS * HEAD_DIM + N_NSA_BRANCH * NSA_HEADS
    kv_cols = 6 * NSA_KV_GROUPS * HEAD_DIM
    return {
        "x": jax.random.normal(ks[0], (BATCH, SEQ, D_MODEL), f32),
        "p": jax.random.normal(ks[1], (DEPTH, BATCH, SEQ, D_PLE), f32),
        "a_w_in": nrm(ks[2], (N_A_LAYERS, D_MODEL, a_in_cols), D_MODEL),
        "a_w_out": nrm(ks[3], (N_A_LAYERS, a_out_rows, D_MODEL), a_out_rows),
        "b_w_qg": nrm(ks[4], (N_B_LAYERS, D_MODEL, b_q_cols), D_MODEL),
        "b_w_out": nrm(ks[5], (N_B_LAYERS, NSA_HEADS * HEAD_DIM, D_MODEL), NSA_HEADS * HEAD_DIM),
        "kv_norm_g": gain(ks[6], (D_MODEL,)),
        "w_kv_shared": nrm(ks[7], (D_MODEL, kv_cols), D_MODEL),
        "cmp_pe_k": 0.1 * jax.random.normal(ks[8], (CMP_LEN, HEAD_DIM), f32),
        "cmp_w1_k": nrm(ks[9], (CMP_LEN * HEAD_DIM, CMP_HIDDEN), CMP_LEN * HEAD_DIM),
        "cmp_w2_k": nrm(ks[10], (CMP_HIDDEN, HEAD_DIM), CMP_HIDDEN, 2.0),
        "cmp_pe_v": 0.1 * jax.random.normal(ks[11], (CMP_LEN, HEAD_DIM), f32),
        "cmp_w1_v": nrm(ks[12], (CMP_LEN * HEAD_DIM, CMP_HIDDEN), CMP_LEN * HEAD_DIM),
        "cmp_w2_v": nrm(ks[13], (CMP_HIDDEN, HEAD_DIM), CMP_HIDDEN, 2.0),
        "attn_norm_g": gain(ks[14], (DEPTH, D_MODEL)),
        "mlp_norm_g": gain(ks[15], (DEPTH, D_MODEL)),
        "mlp_w1": nrm(ks[16], (DEPTH, D_MODEL, D_FF), D_MODEL),
        "mlp_w2": nrm(ks[17], (DEPTH, D_FF, D_MODEL), D_FF, 0.5),
        "ple_norm_g": gain(ks[18], (DEPTH, D_MODEL)),
        "ple_w_gate": nrm(ks[19], (DEPTH, D_MODEL, D_MODEL), D_MODEL),
        "ple_w_proj": nrm(ks[20], (DEPTH, D_PLE, D_MODEL), D_PLE, 0.5),
        "final_norm_g": gain(ks[21], (D_MODEL,)),
    }


def reference(x, p, a_w_in, a_w_out, b_w_qg, b_w_out, kv_norm_g, w_kv_shared,
              cmp_pe_k, cmp_w1_k, cmp_w2_k, cmp_pe_v, cmp_w1_v, cmp_w2_v,
              attn_norm_g, mlp_norm_g, mlp_w1, mlp_w2, ple_norm_g, ple_w_gate, ple_w_proj,
              final_norm_g):
    h = x
    shared = None
    for i in range(DEPTH):
        hn = rmsnorm(h, attn_norm_g[i])
        if i < N_A_LAYERS:
            h = h + dilated_mixture_attention(hn, a_w_in[i], a_w_out[i])
        else:
            if i == N_A_LAYERS:
                shared = nsa_shared_kv(rmsnorm(h, kv_norm_g), w_kv_shared,
                                       cmp_pe_k, cmp_w1_k, cmp_w2_k, cmp_pe_v, cmp_w1_v, cmp_w2_v)
            j = i - N_A_LAYERS
            h = h + nsa_attention(hn, b_w_qg[j], b_w_out[j], *shared)
        hn = rmsnorm(h, mlp_norm_g[i])
        h = h + jnp.square(jax.nn.relu(hn @ mlp_w1[i])) @ mlp_w2[i]
        gate = jax.nn.sigmoid(rmsnorm(h, ple_norm_g[i]) @ ple_w_gate[i])
        h = h + (p[i] @ ple_w_proj[i]) * gate
    return rmsnorm(h, final_norm_g)
```

```python
import functools
import math

import jax
import jax.numpy as jnp
import numpy as np
from jax import lax
from jax.experimental import pallas as pl
from jax.experimental.pallas import tpu as pltpu

F32 = jnp.float32
BF16 = jnp.bfloat16

HEAD_DIM = 128
ROT_DIM = HEAD_DIM // 4
ROPE_THETA = 500000.0
NORM_EPS = 1e-6
NEG_INF = -1e30
FORCE_SCORE = 1e9
TINY = 1e-20
REMOVED_SCORE = -3.0e38

DILATED_GROUPS = ((128, 1), (512, 4), (2048, 16))
NSA_KV_GROUPS = 4
CMP_LEN = 32
CMP_STRIDE = 16
SLC_LEN = 64
SLC_TOP_N = 16
WIN_LEN = 512
N_NSA_BRANCH = 3

LANES = 128
MIB = 1024 * 1024


def _cparams(n_grid, vmem_mib):
    return pltpu.CompilerParams(dimension_semantics=("arbitrary",) * n_grid,
                                vmem_limit_bytes=vmem_mib * MIB)


def _nt_dot(a, b):
    return lax.dot_general(a, b, (((1,), (1,)), ((), ())), preferred_element_type=F32)


def _rope_table(pos, scale, identity=False):
    n = pos.shape[0]
    half = ROT_DIM // 2
    if identity:
        cos = jnp.full((n, LANES), scale, F32)
        z = jnp.zeros((n, LANES), F32)
        return jnp.stack([cos, z, z])
    inv = ROPE_THETA ** (-jnp.arange(half, dtype=F32) * (2.0 / ROT_DIM))
    ang = pos.astype(F32)[:, None] * inv[None, :]
    c, s = jnp.cos(ang), jnp.sin(ang)
    ones = jnp.ones((n, LANES - ROT_DIM), F32)
    zeros_h = jnp.zeros((n, half), F32)
    zeros_r = jnp.zeros((n, LANES - ROT_DIM), F32)
    cos = jnp.concatenate([c, c, ones], axis=1)
    sa = jnp.concatenate([-s, zeros_h, zeros_r], axis=1)
    sb = jnp.concatenate([zeros_h, s, zeros_r], axis=1)
    return jnp.stack([cos, sa, sb]) * scale


def _apply_rope(a, cos, sa, sb):
    return (a * cos + pltpu.roll(a, LANES - ROT_DIM // 2, 1) * sa
            + pltpu.roll(a, ROT_DIM // 2, 1) * sb)


def _norm_matmul_kernel(x_ref, g_ref, w_ref, *rest, rope, act):
    if rope:
        tab_ref, o_ref, xn_ref = rest
    else:
        o_ref, xn_ref = rest

    @pl.when(pl.program_id(1) == 0)
    def _():
        x = x_ref[...]
        ms = jnp.mean(x * x, axis=-1, keepdims=True)
        xn_ref[...] = (x * lax.rsqrt(ms + NORM_EPS) * g_ref[...]).astype(BF16)

    acc = jnp.dot(xn_ref[...], w_ref[...], preferred_element_type=F32)
    if rope:
        cos, sa, sb = tab_ref[0, 0], tab_ref[0, 1], tab_ref[0, 2]
        for c in range(acc.shape[1] // LANES):
            a = acc[:, c * LANES:(c + 1) * LANES]
            o_ref[:, c * LANES:(c + 1) * LANES] = _apply_rope(a, cos, sa, sb).astype(o_ref.dtype)
    elif act == "sigmoid":
        o_ref[...] = (1.0 / (1.0 + jnp.exp(-acc))).astype(o_ref.dtype)
    else:
        o_ref[...] = acc.astype(o_ref.dtype)


def _norm_matmul(x, g, w, *, out_dtype, tables=None, tab_fn=None, seq=None, act=None,
                 tm=512, tn=512, name):
    m, k = x.shape
    n = w.shape[1]
    tm, tn = min(tm, m), min(tn, n)
    in_specs = [pl.BlockSpec((tm, k), lambda i, j: (i, 0)),
                pl.BlockSpec((1, k), lambda i, j: (0, 0)),
                pl.BlockSpec((k, tn), lambda i, j: (0, j))]
    args = [x, g.reshape(1, k), w]
    if tables is not None:
        nsb = seq // tm
        in_specs.append(pl.BlockSpec((1, 3, tm, LANES), lambda i, j: (tab_fn(j), 0, i % nsb, 0)))
        args.append(tables)
    return pl.pallas_call(
        functools.partial(_norm_matmul_kernel, rope=tables is not None, act=act),
        grid=(m // tm, n // tn),
        in_specs=in_specs,
        out_specs=pl.BlockSpec((tm, tn), lambda i, j: (i, j)),
        out_shape=jax.ShapeDtypeStruct((m, n), out_dtype),
        scratch_shapes=[pltpu.VMEM((tm, k), BF16)],
        compiler_params=_cparams(2, 40),
        name=name,
    )(*args)


def _matmul_resid_kernel(a_ref, w_ref, r_ref, o_ref):
    o_ref[...] = r_ref[...] + jnp.dot(a_ref[...], w_ref[...], preferred_element_type=F32)


def _matmul_resid(a, w, r, *, tm=512, tn=1024, name):
    m, k = a.shape
    n = w.shape[1]
    tm, tn = min(tm, m), min(tn, n)
    return pl.pallas_call(
        _matmul_resid_kernel,
        grid=(m // tm, n // tn),
        in_specs=[pl.BlockSpec((tm, k), lambda i, j: (i, 0)),
                  pl.BlockSpec((k, tn), lambda i, j: (0, j)),
                  pl.BlockSpec((tm, tn), lambda i, j: (i, j))],
        out_specs=pl.BlockSpec((tm, tn), lambda i, j: (i, j)),
        out_shape=jax.ShapeDtypeStruct((m, n), F32),
        compiler_params=_cparams(2, 40),
        name=name,
    )(a, w, r)


def _mlp_kernel(x_ref, g_ref, w1_ref, w2_ref, o_ref, xn_ref, acc_ref):
    f = pl.program_id(1)

    @pl.when(f == 0)
    def _():
        x = x_ref[...]
        ms = jnp.mean(x * x, axis=-1, keepdims=True)
        xn_ref[...] = (x * lax.rsqrt(ms + NORM_EPS) * g_ref[...]).astype(BF16)
        acc_ref[...] = jnp.zeros_like(acc_ref)

    a = jnp.dot(xn_ref[...], w1_ref[...], preferred_element_type=F32)
    a = jnp.square(jnp.maximum(a, 0.0)).astype(BF16)
    acc_ref[...] += jnp.dot(a, w2_ref[...], preferred_element_type=F32)

    @pl.when(f == pl.num_programs(1) - 1)
    def _():
        o_ref[...] = x_ref[...] + acc_ref[...]


def _mlp(h, g, w1, w2, *, tm=512, tf=1024, name):
    m, d = h.shape
    ff = w1.shape[1]
    tm, tf = min(tm, m), min(tf, ff)
    return pl.pallas_call(
        _mlp_kernel,
        grid=(m // tm, ff // tf),
        in_specs=[pl.BlockSpec((tm, d), lambda i, f: (i, 0)),
                  pl.BlockSpec((1, d), lambda i, f: (0, 0)),
                  pl.BlockSpec((d, tf), lambda i, f: (0, f)),
                  pl.BlockSpec((tf, d), lambda i, f: (f, 0))],
        out_specs=pl.BlockSpec((tm, d), lambda i, f: (i, 0)),
        out_shape=jax.ShapeDtypeStruct((m, d), F32),
        scratch_shapes=[pltpu.VMEM((tm, d), BF16), pltpu.VMEM((tm, d), F32)],
        compiler_params=_cparams(2, 56),
        name=name,
    )(h, g.reshape(1, d), w1, w2)


def _ple_kernel(x_ref, g_ref, wg_ref, p_ref, wp_ref, r_ref, o_ref, xn_ref):
    @pl.when(pl.program_id(1) == 0)
    def _():
        x = x_ref[...]
        ms = jnp.mean(x * x, axis=-1, keepdims=True)
        xn_ref[...] = (x * lax.rsqrt(ms + NORM_EPS) * g_ref[...]).astype(BF16)

    z = jnp.dot(xn_ref[...], wg_ref[...], preferred_element_type=F32)
    gate = 1.0 / (1.0 + jnp.exp(-z))
    pp = jnp.dot(p_ref[...].astype(BF16), wp_ref[...], preferred_element_type=F32)
    o_ref[...] = r_ref[...] + pp * gate


def _ple(h, g, wg, p, wp, *, tm=512, tn=1024, name):
    m, d = h.shape
    dp = p.shape[1]
    n = wg.shape[1]
    tm, tn = min(tm, m), min(tn, n)
    return pl.pallas_call(
        _ple_kernel,
        grid=(m // tm, n // tn),
        in_specs=[pl.BlockSpec((tm, d), lambda i, j: (i, 0)),
                  pl.BlockSpec((1, d), lambda i, j: (0, 0)),
                  pl.BlockSpec((d, tn), lambda i, j: (0, j)),
                  pl.BlockSpec((tm, dp), lambda i, j: (i, 0)),
                  pl.BlockSpec((dp, tn), lambda i, j: (0, j)),
                  pl.BlockSpec((tm, tn), lambda i, j: (i, j))],
        out_specs=pl.BlockSpec((tm, tn), lambda i, j: (i, j)),
        out_shape=jax.ShapeDtypeStruct((m, n), F32),
        scratch_shapes=[pltpu.VMEM((tm, d), BF16)],
        compiler_params=_cparams(2, 48),
        name=name,
    )(h, g.reshape(1, d), wg, p, wp, h)


def _rmsnorm_kernel(x_ref, g_ref, o_ref):
    x = x_ref[...]
    ms = jnp.mean(x * x, axis=-1, keepdims=True)
    o_ref[...] = x * lax.rsqrt(ms + NORM_EPS) * g_ref[...]


def _rmsnorm(h, g, *, tm=512, name):
    m, d = h.shape
    tm = min(tm, m)
    return pl.pallas_call(
        _rmsnorm_kernel,
        grid=(m // tm,),
        in_specs=[pl.BlockSpec((tm, d), lambda i: (i, 0)),
                  pl.BlockSpec((1, d), lambda i: (0, 0))],
        out_specs=pl.BlockSpec((tm, d), lambda i: (i, 0)),
        out_shape=jax.ShapeDtypeStruct((m, d), F32),
        compiler_params=_cparams(1, 32),
        name=name,
    )(h, g.reshape(1, d))


def _dil_attn_kernel(q_ref, kp_ref, kc_ref, vp_ref, vc_ref, o_ref, lse_ref, *, tiles_per_seq, reach, n_heads):
    g = pl.program_id(0)
    i = pl.program_id(1)
    t = q_ref.shape[1]
    tps = jnp.int32(tiles_per_seq[-1])
    for gi in range(len(tiles_per_seq) - 2, -1, -1):
        tps = jnp.where(g == gi, jnp.int32(tiles_per_seq[gi]), tps)
    first = (i % tps) == 0
    row = lax.broadcasted_iota(jnp.int32, (t, 2 * t), 0)
    col = lax.broadcasted_iota(jnp.int32, (t, 2 * t), 1)
    dist = row + t - col
    prev_ok = jnp.logical_not(first)
    valid = (dist >= 0) & (dist <= reach) & ((col >= t) | prev_ok)
    for h in range(n_heads):
        sl = slice(h * LANES, (h + 1) * LANES)
        q = q_ref[0, :, sl]
        k = jnp.concatenate([kp_ref[0, :, sl], kc_ref[0, :, sl]], axis=0)
        v = jnp.concatenate([vp_ref[0, :, sl], vc_ref[0, :, sl]], axis=0)
        s = jnp.where(valid, _nt_dot(q, k), NEG_INF)
        m = jnp.max(s, axis=-1, keepdims=True)
        p = jnp.exp(s - m)
        l = jnp.sum(p, axis=-1, keepdims=True)
        o = jnp.dot(p.astype(BF16), v, preferred_element_type=F32) / l
        o_ref[0, :, sl] = o
        lse_ref[0, :, sl] = jnp.broadcast_to(m + jnp.log(l), (t, LANES))


def _dil_attn(pqkv, *, tiles_per_seq, reach, n_heads, t, name):
    ng, rows, _ = pqkv.shape
    w = n_heads * LANES
    prev = lambda g, i: (g, jnp.maximum(i - 1, 0), 1)
    out_sds = jax.ShapeDtypeStruct((ng, rows, w), F32)
    return pl.pallas_call(
        functools.partial(_dil_attn_kernel, tiles_per_seq=tiles_per_seq, reach=reach, n_heads=n_heads),
        grid=(ng, rows // t),
        in_specs=[pl.BlockSpec((1, t, w), lambda g, i: (g, i, 0)),
                  pl.BlockSpec((1, t, w), lambda g, i: (g, jnp.maximum(i - 1, 0), 1)),
                  pl.BlockSpec((1, t, w), lambda g, i: (g, i, 1)),
                  pl.BlockSpec((1, t, w), lambda g, i: (g, jnp.maximum(i - 1, 0), 2)),
                  pl.BlockSpec((1, t, w), lambda g, i: (g, i, 2))],
        out_specs=[pl.BlockSpec((1, t, w), lambda g, i: (g, i, 0)),
                   pl.BlockSpec((1, t, w), lambda g, i: (g, i, 0))],
        out_shape=[out_sds, out_sds],
        compiler_params=_cparams(2, 40),
        name=name,
    )(pqkv, pqkv, pqkv, pqkv, pqkv)


def _combine_kernel(o_ref, lse_ref, out_ref):
    ng = o_ref.shape[0]
    m = lse_ref[0]
    for g in range(1, ng):
        m = jnp.maximum(m, lse_ref[g])
    num = jnp.zeros_like(m)
    den = jnp.zeros_like(m)
    for g in range(ng):
        e = jnp.exp(lse_ref[g] - m)
        den = den + e
        num = num + e * o_ref[g]
    out_ref[...] = (num / den).astype(out_ref.dtype)


def _combine(o, lse, *, tm=512, name):
    ng, m, w = o.shape
    tm = min(tm, m)
    return pl.pallas_call(
        _combine_kernel,
        grid=(m // tm,),
        in_specs=[pl.BlockSpec((ng, tm, w), lambda i: (0, i, 0)),
                  pl.BlockSpec((ng, tm, w), lambda i: (0, i, 0))],
        out_specs=pl.BlockSpec((tm, w), lambda i: (i, 0)),
        out_shape=jax.ShapeDtypeStruct((m, w), BF16),
        compiler_params=_cparams(1, 40),
        name=name,
    )(o, lse)


def _cmp_kernel(z_ref, w1_ref, w2_ref, pe_ref, tab_ref, o_ref, *, n_cmp):
    z = z_ref[0, 0]
    w1 = w1_ref[0]
    bias = jnp.dot(pe_ref[0].astype(BF16), w1, preferred_element_type=F32)[0:1]
    hid = jnp.dot(z, w1, preferred_element_type=F32) + bias
    hid = 0.5 * hid * (1.0 + jnp.tanh(math.sqrt(2.0 / math.pi) * (hid + 0.044715 * (hid * hid * hid))))
    o = jnp.dot(hid.astype(BF16), w2_ref[0], preferred_element_type=F32)
    o = _apply_rope(o, tab_ref[0, 0], tab_ref[0, 1], tab_ref[0, 2])
    row = lax.broadcasted_iota(jnp.int32, o.shape, 0)
    o_ref[0, 0] = jnp.where(row < n_cmp, o, 0.0).astype(o_ref.dtype)


def _compress(z, w1, w2, pe, tab, *, n_cmp, name):
    two, bg, nc, ld = z.shape
    hid = w1.shape[2]
    return pl.pallas_call(
        functools.partial(_cmp_kernel, n_cmp=n_cmp),
        grid=(two, bg),
        in_specs=[pl.BlockSpec((1, 1, nc, ld), lambda s, b: (s, b, 0, 0)),
                  pl.BlockSpec((1, ld, hid), lambda s, b: (s, 0, 0)),
                  pl.BlockSpec((1, hid, LANES), lambda s, b: (s, 0, 0)),
                  pl.BlockSpec((1, 8, ld), lambda s, b: (s, 0, 0)),
                  pl.BlockSpec((1, 3, nc, LANES), lambda s, b: (s, 0, 0, 0))],
        out_specs=pl.BlockSpec((1, 1, nc, LANES), lambda s, b: (s, b, 0, 0)),
        out_shape=jax.ShapeDtypeStruct((two, bg, nc, LANES), BF16),
        compiler_params=_cparams(2, 40),
        name=name,
    )(z, w1, w2, pe, tab)


def _nsa_kernel(q_ref, gate_ref, kc_ref, vc_ref, ks_ref, vs_ref, kw_ref, vw_ref, ov_ref, o_ref,
                *, tq, tk, hpg, top_n):
    qi = pl.program_id(2)
    t0 = qi * tq
    r = hpg * tq
    q = q_ref[0]
    qs = jnp.concatenate([q[:, h * LANES:(h + 1) * LANES] for h in range(hpg)], axis=0)
    trow = t0 + (lax.broadcasted_iota(jnp.int32, (r, 1), 0) & (tq - 1))

    nc = kc_ref.shape[2]
    s = _nt_dot(qs, kc_ref[0, 0])
    cend = lax.broadcasted_iota(jnp.int32, (1, nc), 1) * CMP_STRIDE + (CMP_LEN - 1)
    vc_mask = cend <= trow
    s = jnp.where(vc_mask, s, NEG_INF)
    m = jnp.max(s, axis=-1, keepdims=True)
    e = jnp.where(vc_mask, jnp.exp(s - m), 0.0)
    p = e / jnp.maximum(jnp.sum(e, axis=-1, keepdims=True), TINY)
    o_cmp = jnp.dot(p.astype(BF16), vc_ref[0, 0], preferred_element_type=F32)

    psum = p[0:tq]
    for h in range(1, hpg):
        psum = psum + p[h * tq:(h + 1) * tq]
    ov = ov_ref[...]
    p_hi = psum.astype(BF16)
    rem = psum - p_hi.astype(F32)
    p_mid = rem.astype(BF16)
    p_lo = (rem - p_mid.astype(F32)).astype(BF16)
    imp = (jnp.dot(p_hi, ov, preferred_element_type=F32) + jnp.dot(p_mid, ov, preferred_element_type=F32)
           + jnp.dot(p_lo, ov, preferred_element_type=F32))

    nj = ov.shape[1]
    jl = lax.broadcasted_iota(jnp.int32, (tq, nj), 1)
    tqi = t0 + lax.broadcasted_iota(jnp.int32, (tq, nj), 0)
    cur = tqi // SLC_LEN
    forced = (jl == 0) | (jl == cur) | (jl == cur - 1)
    causal = jl * SLC_LEN <= tqi
    score_t = jnp.where(forced, FORCE_SCORE, jnp.where(causal, imp, NEG_INF)).T
    jrow = lax.broadcasted_iota(jnp.int32, (nj, tq), 0)
    bias_t = jnp.full((nj, tq), NEG_INF, F32)
    for _ in range(top_n):
        mx = jnp.max(score_t, axis=0, keepdims=True)
        first = jnp.min(jnp.where(score_t == mx, jrow, nj), axis=0, keepdims=True)
        pick = jrow == first
        bias_t = jnp.where(pick, 0.0, bias_t)
        score_t = jnp.where(pick, REMOVED_SCORE, score_t)
    bias = bias_t.T.astype(BF16)
    q_aug = jnp.concatenate([qs, jnp.concatenate([bias] * hpg, axis=0)], axis=1)

    def slc_step(kj, carry):
        m_i, l_i, acc = carry
        k0 = pl.multiple_of(kj * tk, tk)
        k = ks_ref[0, pl.ds(k0, tk), :]
        v = vs_ref[0, pl.ds(k0, tk), :]
        kblk = (k0 + lax.broadcasted_iota(jnp.int32, (tk, nj), 0)) // SLC_LEN
        onehot = jnp.where(kblk == lax.broadcasted_iota(jnp.int32, (tk, nj), 1), 1.0, 0.0).astype(BF16)
        sc = _nt_dot(q_aug, jnp.concatenate([k, onehot], axis=1))
        tok = k0 + lax.broadcasted_iota(jnp.int32, (1, tk), 1)
        sc = jnp.where(tok <= trow, sc, NEG_INF)
        m_new = jnp.maximum(m_i, jnp.max(sc, axis=-1, keepdims=True))
        alpha = jnp.exp(m_i - m_new)
        pr = jnp.exp(sc - m_new)
        l_new = alpha * l_i + jnp.sum(pr, axis=-1, keepdims=True)
        acc = alpha * acc + jnp.dot(pr.astype(BF16), v, preferred_element_type=F32)
        return m_new, l_new, acc

    n_kt = (t0 + tq - 1) // tk + 1
    init = (jnp.full((r, 1), NEG_INF, F32), jnp.zeros((r, 1), F32), jnp.zeros((r, LANES), F32))
    _, l_s, acc_s = lax.fori_loop(0, n_kt, slc_step, init)
    o_slc = acc_s / l_s

    span = WIN_LEN + tq
    start = pl.multiple_of(jnp.maximum(t0 - WIN_LEN, 0), tq)
    kw = kw_ref[0, pl.ds(start, span), :]
    vw = vw_ref[0, pl.ds(start, span), :]
    sw = _nt_dot(qs, kw)
    dist = trow - (start + lax.broadcasted_iota(jnp.int32, (1, span), 1))
    sw = jnp.where((dist >= 0) & (dist < WIN_LEN), sw, NEG_INF)
    mw = jnp.max(sw, axis=-1, keepdims=True)
    pw = jnp.exp(sw - mw)
    o_win = jnp.dot(pw.astype(BF16), vw, preferred_element_type=F32) / jnp.sum(pw, axis=-1, keepdims=True)

    gt = gate_ref[0]

    def gcol(br):
        return jnp.concatenate([gt[:, br * hpg + h:br * hpg + h + 1] for h in range(hpg)], axis=0)

    o = gcol(0) * o_cmp + gcol(1) * o_slc + gcol(2) * o_win
    o_ref[0] = jnp.concatenate([o[h * tq:(h + 1) * tq] for h in range(hpg)], axis=1).astype(o_ref.dtype)


def _nsa(q, gates, kv, cmp_kv, overlap, *, n_groups, hpg, top_n, tq=128, tk=512, name):
    b, s, _ = q.shape
    nc = cmp_kv.shape[2]
    nj = overlap.shape[1]
    tk = min(tk, s)
    g_ = n_groups
    seg = lambda sidx: (lambda bi, gi, qi: (bi, 0, sidx * g_ + gi))
    cmp_spec = lambda sidx: pl.BlockSpec((1, 1, nc, LANES), lambda bi, gi, qi: (sidx, bi * g_ + gi, 0, 0))
    return pl.pallas_call(
        functools.partial(_nsa_kernel, tq=tq, tk=tk, hpg=hpg, top_n=top_n),
        grid=(b, g_, s // tq),
        in_specs=[pl.BlockSpec((1, tq, hpg * LANES), lambda bi, gi, qi: (bi, qi, gi)),
                  pl.BlockSpec((1, tq, LANES), lambda bi, gi, qi: (bi, qi, gi)),
                  cmp_spec(0), cmp_spec(1),
                  pl.BlockSpec((1, s, LANES), seg(2)),
                  pl.BlockSpec((1, s, LANES), seg(3)),
                  pl.BlockSpec((1, s, LANES), seg(4)),
                  pl.BlockSpec((1, s, LANES), seg(5)),
                  pl.BlockSpec((nc, nj), lambda bi, gi, qi: (0, 0))],
        out_specs=pl.BlockSpec((1, tq, hpg * LANES), lambda bi, gi, qi: (bi, qi, gi)),
        out_shape=jax.ShapeDtypeStruct(q.shape, BF16),
        compiler_params=_cparams(3, 48),
        name=name,
    )(q, gates, cmp_kv, cmp_kv, kv, kv, kv, kv, overlap)


def _dilated_layer(h, g_attn, w_in, w_out, batch, seq):
    m, d = h.shape
    ng = len(DILATED_GROUPS)
    hg = w_out.shape[0] // HEAD_DIM
    gw = 3 * hg * HEAD_DIM
    reach = DILATED_GROUPS[0][0] // DILATED_GROUPS[0][1]
    assert all(wd // dl == reach for wd, dl in DILATED_GROUPS)
    w = w_in.reshape(d, 3, ng, hg * HEAD_DIM).transpose(0, 2, 1, 3).reshape(d, ng * gw).astype(BF16)
    pos = jnp.arange(seq)
    tables = jnp.stack([_rope_table(pos, HEAD_DIM ** -0.5), _rope_table(pos, 1.0),
                        _rope_table(pos, 1.0, identity=True)])
    tn = 512
    per_part = hg * HEAD_DIM // tn
    qkv = _norm_matmul(h, g_attn, w, out_dtype=BF16, tables=tables, seq=seq,
                       tab_fn=lambda j: (j % (3 * per_part)) // per_part, tn=tn, name="a_qkv")
    parts = []
    for gi, (_, dil) in enumerate(DILATED_GROUPS):
        blk = qkv[:, gi * gw:(gi + 1) * gw].reshape(batch, seq // dil, dil, gw)
        parts.append(blk.transpose(0, 2, 1, 3).reshape(m, gw))
    t = min(256, seq // max(dl for _, dl in DILATED_GROUPS))
    tiles_per_seq = tuple(seq // dil // t for _, dil in DILATED_GROUPS)
    o, lse = _dil_attn(jnp.stack(parts), tiles_per_seq=tiles_per_seq, reach=reach, n_heads=hg, t=t,
                       name="a_attn")
    wd = hg * HEAD_DIM

    def unperm(a):
        outs = []
        for gi, (_, dil) in enumerate(DILATED_GROUPS):
            outs.append(a[gi].reshape(batch, dil, seq // dil, wd).transpose(0, 2, 1, 3).reshape(m, wd))
        return jnp.stack(outs)

    oc = _combine(unperm(o), unperm(lse), name="a_combine")
    return _matmul_resid(oc, w_out.astype(BF16), h, name="a_out")


def _nsa_shared_kv(h, kv_norm_g, w_kv, cmp_k, cmp_v, batch, seq):
    g_ = NSA_KV_GROUPS
    pos = jnp.arange(seq)
    ident = _rope_table(pos, 1.0, identity=True)
    kv_tables = jnp.stack([_rope_table(pos, 1.0), ident])
    seg_w = g_ * HEAD_DIM
    kv = _norm_matmul(h, kv_norm_g, w_kv.astype(BF16), out_dtype=BF16, tables=kv_tables, seq=seq,
                      tab_fn=lambda j: jnp.where((j == 2) | (j == 4), 0, 1), tn=seg_w, name="b_kv")
    n_chunk = seq // CMP_STRIDE
    n_cmp = (seq - CMP_LEN) // CMP_STRIDE + 1
    assert CMP_LEN == 2 * CMP_STRIDE

    def blocks(seg_idx):
        c = kv[:, seg_idx * seg_w:(seg_idx + 1) * seg_w].reshape(batch, n_chunk, CMP_STRIDE, g_, HEAD_DIM)
        c = c.transpose(0, 3, 1, 2, 4).reshape(batch * g_, n_chunk, CMP_STRIDE * HEAD_DIM)
        return jnp.concatenate([c, jnp.roll(c, -1, axis=1)], axis=-1)

    z = jnp.stack([blocks(0), blocks(1)])
    (pe_k, w1_k, w2_k), (pe_v, w1_v, w2_v) = cmp_k, cmp_v
    pe = jnp.stack([pe_k.reshape(1, -1), pe_v.reshape(1, -1)])
    pe = jnp.broadcast_to(pe, (2, 8, pe.shape[-1]))
    cmp_tab = jnp.stack([_rope_table(jnp.arange(n_chunk) * CMP_STRIDE + CMP_LEN - 1, 1.0),
                         _rope_table(jnp.arange(n_chunk), 1.0, identity=True)])
    cmp_kv = _compress(z, jnp.stack([w1_k, w1_v]).astype(BF16), jnp.stack([w2_k, w2_v]).astype(BF16),
                       pe, cmp_tab, n_cmp=n_cmp, name="b_compress")
    return kv, cmp_kv


def _nsa_layer(h, g_attn, w_qg, w_out, shared, batch, seq):
    m, d = h.shape
    g_ = NSA_KV_GROUPS
    n_heads = w_out.shape[0] // HEAD_DIM
    hpg = n_heads // g_
    seg_w = g_ * HEAD_DIM
    pos = jnp.arange(seq)
    kv, cmp_kv = shared
    n_chunk = seq // CMP_STRIDE
    n_cmp = (seq - CMP_LEN) // CMP_STRIDE + 1
    n_q = n_heads * HEAD_DIM
    q_tables = jnp.stack([_rope_table(pos, HEAD_DIM ** -0.5)])
    q = _norm_matmul(h, g_attn, w_qg[:, :n_q].astype(BF16), out_dtype=BF16, tables=q_tables, seq=seq,
                     tab_fn=lambda j: 0, tn=512, name="b_q")
    wg = w_qg[:, n_q:].reshape(d, N_NSA_BRANCH, g_, hpg).transpose(0, 2, 1, 3).reshape(d, g_, N_NSA_BRANCH * hpg)
    wg = jnp.pad(wg, ((0, 0), (0, 0), (0, LANES - N_NSA_BRANCH * hpg))).reshape(d, g_ * LANES)
    gates = _norm_matmul(h, g_attn, wg.astype(BF16), out_dtype=F32, act="sigmoid", tn=g_ * LANES,
                         name="b_gates")

    n_slc = seq // SLC_LEN
    nj = -(-n_slc // LANES) * LANES
    cs = np.arange(n_chunk)[:, None] * CMP_STRIDE
    ss = np.arange(nj)[None, :] * SLC_LEN
    ov = np.clip(np.minimum(cs + CMP_LEN, ss + SLC_LEN) - np.maximum(cs, ss), 0, None) / CMP_LEN
    ov[n_cmp:, :] = 0.0
    overlap = jnp.asarray(ov, BF16)

    o = _nsa(q.reshape(batch, seq, n_q), gates.reshape(batch, seq, g_ * LANES),
             kv.reshape(batch, seq, 6 * seg_w), cmp_kv, overlap,
             n_groups=g_, hpg=hpg, top_n=min(SLC_TOP_N, n_slc), name="b_nsa")
    return _matmul_resid(o.reshape(m, n_q), w_out.astype(BF16), h, name="b_out")


def kernel(x, p, a_w_in, a_w_out, b_w_qg, b_w_out, kv_norm_g, w_kv_shared, cmp_pe_k, cmp_w1_k, cmp_w2_k,
           cmp_pe_v, cmp_w1_v, cmp_w2_v, attn_norm_g, mlp_norm_g, mlp_w1, mlp_w2, ple_norm_g, ple_w_gate,
           ple_w_proj, final_norm_g):
    batch, seq, d = x.shape
    m = batch * seq
    depth = attn_norm_g.shape[0]
    n_a = a_w_in.shape[0]
    h = x.reshape(m, d)
    shared = None
    for i in range(depth):
        if i < n_a:
            h = _dilated_layer(h, attn_norm_g[i], a_w_in[i], a_w_out[i], batch, seq)
        else:
            if i == n_a:
                shared = _nsa_shared_kv(h, kv_norm_g, w_kv_shared, (cmp_pe_k, cmp_w1_k, cmp_w2_k),
                                        (cmp_pe_v, cmp_w1_v, cmp_w2_v), batch, seq)
            j = i - n_a
            h = _nsa_layer(h, attn_norm_g[i], b_w_qg[j], b_w_out[j], shared, batch, seq)
        h = _mlp(h, mlp_norm_g[i], mlp_w1[i].astype(BF16), mlp_w2[i].astype(BF16), name=f"mlp{i}")
        h = _ple(h, ple_norm_g[i], ple_w_gate[i].astype(BF16), p[i].reshape(m, -1),
                 ple_w_proj[i].astype(BF16), name=f"ple{i}")
    return _rmsnorm(h, final_norm_g, name="final_norm").reshape(batch, seq, d)
```

```python
import functools
import math

import jax
import jax.numpy as jnp
import numpy as np
from jax import lax
from jax.experimental import pallas as pl
from jax.experimental.pallas import tpu as pltpu

F32 = jnp.float32
BF16 = jnp.bfloat16

HEAD_DIM = 128
ROT_DIM = HEAD_DIM // 4
ROPE_THETA = 500000.0
NORM_EPS = 1e-6
NEG_INF = -1e30
FORCE_SCORE = 1e9
TINY = 1e-20
REMOVED_SCORE = -3.0e38
LOG2E = math.log2(math.e)

DILATED_GROUPS = ((128, 1), (512, 4), (2048, 16))
NSA_KV_GROUPS = 4
CMP_LEN = 32
CMP_STRIDE = 16
SLC_LEN = 64
SLC_TOP_N = 16
WIN_LEN = 512
N_NSA_BRANCH = 3

LANES = 128
SUBLANES = 8
MIB = 1024 * 1024


def _cparams(n_grid, vmem_mib):
    return pltpu.CompilerParams(dimension_semantics=("arbitrary",) * n_grid,
                                vmem_limit_bytes=vmem_mib * MIB)


def _nt_dot(a, b):
    return lax.dot_general(a, b, (((1,), (1,)), ((), ())), preferred_element_type=F32)


def _dot(a, b):
    return jnp.dot(a, b, preferred_element_type=F32)


def _perm_head_cols(w):
    k, n = w.shape
    half = ROT_DIM // 2
    w = w.reshape(k, n // HEAD_DIM, HEAD_DIM)
    w = jnp.concatenate([w[..., :half], w[..., ROT_DIM:LANES // 2 + half], w[..., half:ROT_DIM],
                         w[..., LANES // 2 + half:]], axis=-1)
    return w.reshape(k, n)


def _rope_table(pos, scale, identity=False):
    n = pos.shape[0]
    half = ROT_DIM // 2
    if identity:
        return jnp.stack([jnp.full((n, LANES), scale, F32), jnp.zeros((n, LANES), F32)])
    inv = ROPE_THETA ** (-jnp.arange(half, dtype=F32) * (2.0 / ROT_DIM))
    ang = pos.astype(F32)[:, None] * inv[None, :]
    c, s = jnp.cos(ang), jnp.sin(ang)
    ones = jnp.ones((n, LANES // 2 - half), F32)
    zeros = jnp.zeros((n, LANES // 2 - half), F32)
    cos = jnp.concatenate([c, ones, c, ones], axis=1)
    sg = jnp.concatenate([-s, zeros, s, zeros], axis=1)
    return jnp.stack([cos, sg]) * scale


def _apply_rope(a, cos, sg):
    return a * cos + pltpu.roll(a, LANES // 2, 1) * sg


def _rms_scale(x, g):
    ms = jnp.mean(x * x, axis=-1, keepdims=True)
    return (x * lax.rsqrt(ms + NORM_EPS) * g).astype(BF16)


def _norm_matmul_kernel(x_ref, g_ref, w_ref, *rest, rope_fn, act):
    if rope_fn is not None:
        tab_ref, o_ref, xn_ref = rest
    else:
        o_ref, xn_ref = rest
    j = pl.program_id(1)

    @pl.when(j == 0)
    def _():
        xn_ref[...] = _rms_scale(x_ref[...], g_ref[...])

    acc = _dot(xn_ref[...], w_ref[...])
    if rope_fn is None:
        if act == "sigmoid":
            acc = 1.0 / (1.0 + jnp.exp(-acc))
        o_ref[...] = acc.astype(o_ref.dtype)
        return
    is_rope = rope_fn(j)

    @pl.when(is_rope)
    def _():
        cos, sg = tab_ref[0, 0], tab_ref[0, 1]
        for c in range(acc.shape[1] // LANES):
            sl = slice(c * LANES, (c + 1) * LANES)
            o_ref[:, sl] = _apply_rope(acc[:, sl], cos, sg).astype(o_ref.dtype)

    @pl.when(jnp.logical_not(is_rope))
    def _():
        o_ref[...] = acc.astype(o_ref.dtype)


def _norm_matmul(x, g, w, *, out_dtype, tables=None, tab_fn=None, rope_fn=None, seq=None, act=None,
                 tm=512, tn=512, name):
    m, k = x.shape
    n = w.shape[1]
    tm, tn = min(tm, m), min(tn, n)
    in_specs = [pl.BlockSpec((tm, k), lambda i, j: (i, 0)),
                pl.BlockSpec((1, k), lambda i, j: (0, 0)),
                pl.BlockSpec((k, tn), lambda i, j: (0, j))]
    args = [x, g.reshape(1, k), w]
    if tables is not None:
        nsb = seq // tm
        in_specs.append(pl.BlockSpec((1, 2, tm, LANES), lambda i, j: (tab_fn(j), 0, i % nsb, 0)))
        args.append(tables)
    return pl.pallas_call(
        functools.partial(_norm_matmul_kernel, rope_fn=rope_fn, act=act),
        grid=(m // tm, n // tn),
        in_specs=in_specs,
        out_specs=pl.BlockSpec((tm, tn), lambda i, j: (i, j)),
        out_shape=jax.ShapeDtypeStruct((m, n), out_dtype),
        scratch_shapes=[pltpu.VMEM((tm, k), BF16)],
        compiler_params=_cparams(2, 40),
        name=name,
    )(*args)


def _matmul_resid_kernel(a_ref, w_ref, r_ref, o_ref):
    o_ref[...] = r_ref[...] + _dot(a_ref[...], w_ref[...])


def _matmul_resid(a, w, r, *, tm=512, tn=1024, name):
    m, k = a.shape
    n = w.shape[1]
    tm, tn = min(tm, m), min(tn, n)
    return pl.pallas_call(
        _matmul_resid_kernel,
        grid=(m // tm, n // tn),
        in_specs=[pl.BlockSpec((tm, k), lambda i, j: (i, 0)),
                  pl.BlockSpec((k, tn), lambda i, j: (0, j)),
                  pl.BlockSpec((tm, tn), lambda i, j: (i, j))],
        out_specs=pl.BlockSpec((tm, tn), lambda i, j: (i, j)),
        out_shape=jax.ShapeDtypeStruct((m, n), F32),
        compiler_params=_cparams(2, 40),
        name=name,
    )(a, w, r)


def _mlp_kernel(x_ref, g_ref, w1_ref, w2_ref, o_ref, xn_ref, acc_ref):
    f = pl.program_id(1)

    @pl.when(f == 0)
    def _():
        xn_ref[...] = _rms_scale(x_ref[...], g_ref[...])
        acc_ref[...] = jnp.zeros_like(acc_ref)

    a = _dot(xn_ref[...], w1_ref[...])
    a = jnp.square(jnp.maximum(a, 0.0)).astype(BF16)
    acc_ref[...] += _dot(a, w2_ref[...])

    @pl.when(f == pl.num_programs(1) - 1)
    def _():
        o_ref[...] = x_ref[...] + acc_ref[...]


def _mlp(h, g, w1, w2, *, tm=512, tf=1024, name):
    m, d = h.shape
    ff = w1.shape[1]
    tm, tf = min(tm, m), min(tf, ff)
    return pl.pallas_call(
        _mlp_kernel,
        grid=(m // tm, ff // tf),
        in_specs=[pl.BlockSpec((tm, d), lambda i, f: (i, 0)),
                  pl.BlockSpec((1, d), lambda i, f: (0, 0)),
                  pl.BlockSpec((d, tf), lambda i, f: (0, f)),
                  pl.BlockSpec((tf, d), lambda i, f: (f, 0))],
        out_specs=pl.BlockSpec((tm, d), lambda i, f: (i, 0)),
        out_shape=jax.ShapeDtypeStruct((m, d), F32),
        scratch_shapes=[pltpu.VMEM((tm, d), BF16), pltpu.VMEM((tm, d), F32)],
        compiler_params=_cparams(2, 56),
        name=name,
    )(h, g.reshape(1, d), w1, w2)


def _ple_kernel(x_ref, g_ref, wg_ref, p_ref, wp_ref, r_ref, o_ref, xn_ref):
    @pl.when(pl.program_id(1) == 0)
    def _():
        xn_ref[...] = _rms_scale(x_ref[...], g_ref[...])

    z = _dot(xn_ref[...], wg_ref[...])
    gate = 1.0 / (1.0 + jnp.exp(-z))
    pp = _dot(p_ref[...].astype(BF16), wp_ref[...])
    o_ref[...] = r_ref[...] + pp * gate


def _ple(h, g, wg, p, wp, *, tm=512, tn=1024, name):
    m, d = h.shape
    dp = p.shape[1]
    n = wg.shape[1]
    tm, tn = min(tm, m), min(tn, n)
    return pl.pallas_call(
        _ple_kernel,
        grid=(m // tm, n // tn),
        in_specs=[pl.BlockSpec((tm, d), lambda i, j: (i, 0)),
                  pl.BlockSpec((1, d), lambda i, j: (0, 0)),
                  pl.BlockSpec((d, tn), lambda i, j: (0, j)),
                  pl.BlockSpec((tm, dp), lambda i, j: (i, 0)),
                  pl.BlockSpec((dp, tn), lambda i, j: (0, j)),
                  pl.BlockSpec((tm, tn), lambda i, j: (i, j))],
        out_specs=pl.BlockSpec((tm, tn), lambda i, j: (i, j)),
        out_shape=jax.ShapeDtypeStruct((m, n), F32),
        scratch_shapes=[pltpu.VMEM((tm, d), BF16)],
        compiler_params=_cparams(2, 48),
        name=name,
    )(h, g.reshape(1, d), wg, p, wp, h)


def _rmsnorm_kernel(x_ref, g_ref, o_ref):
    x = x_ref[...]
    ms = jnp.mean(x * x, axis=-1, keepdims=True)
    o_ref[...] = x * lax.rsqrt(ms + NORM_EPS) * g_ref[...]


def _rmsnorm(h, g, *, tm=512, name):
    m, d = h.shape
    tm = min(tm, m)
    return pl.pallas_call(
        _rmsnorm_kernel,
        grid=(m // tm,),
        in_specs=[pl.BlockSpec((tm, d), lambda i: (i, 0)),
                  pl.BlockSpec((1, d), lambda i: (0, 0))],
        out_specs=pl.BlockSpec((tm, d), lambda i: (i, 0)),
        out_shape=jax.ShapeDtypeStruct((m, d), F32),
        compiler_params=_cparams(1, 32),
        name=name,
    )(h, g.reshape(1, d))


def _dil_attn_kernel(q_ref, kp_ref, kc_ref, vp_ref, vc_ref, o_ref, lse_ref, *, reach, n_heads):
    i = pl.program_id(2)
    t = q_ref.shape[1]
    tp = kp_ref.shape[1]
    row = lax.broadcasted_iota(jnp.int32, (t, tp + t), 0)
    col = lax.broadcasted_iota(jnp.int32, (t, tp + t), 1)
    dist = row + tp - col
    valid = (dist >= 0) & (dist <= reach) & ((col >= tp) | (i > 0))
    lane = lax.broadcasted_iota(jnp.int32, (t, LANES), 1)
    lse_tile = jnp.zeros((t, LANES), F32)
    for h in range(n_heads):
        sl = slice(h * LANES, (h + 1) * LANES)
        k = jnp.concatenate([kp_ref[0, :, sl], kc_ref[0, :, sl]], axis=0)
        v = jnp.concatenate([vp_ref[0, :, sl], vc_ref[0, :, sl]], axis=0)
        s = jnp.where(valid, _nt_dot(q_ref[0, :, sl], k), NEG_INF)
        m = jnp.max(s, axis=-1, keepdims=True)
        p = jnp.exp2(s - m)
        l = jnp.sum(p, axis=-1, keepdims=True)
        o_ref[0, :, sl] = (_dot(p.astype(BF16), v) * (1.0 / l)).astype(o_ref.dtype)
        lse_tile = jnp.where(lane == h, m + jnp.log2(l), lse_tile)
    lse_ref[0] = lse_tile


def _dil_attn(qkv, *, batch, seq, group, dil, n_groups, reach, n_heads, t, name):
    w = n_heads * LANES
    tp = reach
    assert t % tp == 0 and tp % SUBLANES == 0
    su = seq // dil
    per_tok = n_groups * 3
    view = qkv.reshape(batch, su, dil * per_tok * w)
    col = lambda r, part: r * per_tok + group * 3 + part
    cur = lambda part: (lambda b, r, i: (b, i, col(r, part)))
    prev = lambda part: (lambda b, r, i: (b, jnp.maximum(i * (t // tp) - 1, 0), col(r, part)))
    o, lse = pl.pallas_call(
        functools.partial(_dil_attn_kernel, reach=reach, n_heads=n_heads),
        grid=(batch, dil, su // t),
        in_specs=[pl.BlockSpec((1, t, w), cur(0)),
                  pl.BlockSpec((1, tp, w), prev(1)),
                  pl.BlockSpec((1, t, w), cur(1)),
                  pl.BlockSpec((1, tp, w), prev(2)),
                  pl.BlockSpec((1, t, w), cur(2))],
        out_specs=[pl.BlockSpec((1, t, w), lambda b, r, i: (b, i, r)),
                   pl.BlockSpec((1, t, LANES), lambda b, r, i: (b, i, r))],
        out_shape=[jax.ShapeDtypeStruct((batch, su, dil * w), BF16),
                   jax.ShapeDtypeStruct((batch, su, dil * LANES), F32)],
        compiler_params=_cparams(3, 40),
        name=name,
    )(view, view, view, view, view)
    return o.reshape(batch * seq, w), lse.reshape(batch * seq, LANES)


def _mix_out_kernel(*refs, n_groups, n_heads):
    o_refs = refs[:n_groups]
    lse_refs = refs[n_groups:2 * n_groups]
    w_ref, r_ref, out_ref, oc_ref = refs[2 * n_groups:]

    @pl.when(pl.program_id(1) == 0)
    def _():
        m = lse_refs[0][...]
        for g in range(1, n_groups):
            m = jnp.maximum(m, lse_refs[g][...])
        e = [jnp.exp2(lse_refs[g][...] - m) for g in range(n_groups)]
        den = e[0]
        for g in range(1, n_groups):
            den = den + e[g]
        inv = 1.0 / den
        for h in range(n_heads):
            sl = slice(h * LANES, (h + 1) * LANES)
            acc = (e[0] * inv)[:, h:h + 1] * o_refs[0][:, sl].astype(F32)
            for g in range(1, n_groups):
                acc = acc + (e[g] * inv)[:, h:h + 1] * o_refs[g][:, sl].astype(F32)
            oc_ref[:, sl] = acc.astype(BF16)

    out_ref[...] = r_ref[...] + _dot(oc_ref[...], w_ref[...])


def _mix_out(os_, lses, w, r, *, n_heads, tm=512, tn=1024, name):
    m, k = os_[0].shape
    n = w.shape[1]
    ng = len(os_)
    tm, tn = min(tm, m), min(tn, n)
    return pl.pallas_call(
        functools.partial(_mix_out_kernel, n_groups=ng, n_heads=n_heads),
        grid=(m // tm, n // tn),
        in_specs=([pl.BlockSpec((tm, k), lambda i, j: (i, 0))] * ng
                  + [pl.BlockSpec((tm, LANES), lambda i, j: (i, 0))] * ng
                  + [pl.BlockSpec((k, tn), lambda i, j: (0, j)),
                     pl.BlockSpec((tm, tn), lambda i, j: (i, j))]),
        out_specs=pl.BlockSpec((tm, tn), lambda i, j: (i, j)),
        out_shape=jax.ShapeDtypeStruct((m, n), F32),
        scratch_shapes=[pltpu.VMEM((tm, k), BF16)],
        compiler_params=_cparams(2, 40),
        name=name,
    )(*os_, *lses, w, r)


def _cmp_kernel(z_ref, w1_ref, w2_ref, pe_ref, tab_ref, o_ref, *, n_cmp):
    w1 = w1_ref[0]
    bias = _dot(pe_ref[0].astype(BF16), w1)[0:1]
    hid = _dot(z_ref[0, 0], w1) + bias
    hid = 0.5 * hid * (1.0 + jnp.tanh(math.sqrt(2.0 / math.pi) * (hid + 0.044715 * (hid * hid * hid))))
    o = _apply_rope(_dot(hid.astype(BF16), w2_ref[0]), tab_ref[0, 0], tab_ref[0, 1])
    row = lax.broadcasted_iota(jnp.int32, o.shape, 0)
    o_ref[0, 0] = jnp.where(row < n_cmp, o, 0.0).astype(o_ref.dtype)


def _compress(z, w1, w2, pe, tab, *, n_cmp, name):
    two, bg, nc, ld = z.shape
    hid = w1.shape[2]
    return pl.pallas_call(
        functools.partial(_cmp_kernel, n_cmp=n_cmp),
        grid=(two, bg),
        in_specs=[pl.BlockSpec((1, 1, nc, ld), lambda s, b: (s, b, 0, 0)),
                  pl.BlockSpec((1, ld, hid), lambda s, b: (s, 0, 0)),
                  pl.BlockSpec((1, hid, LANES), lambda s, b: (s, 0, 0)),
                  pl.BlockSpec((1, SUBLANES, ld), lambda s, b: (s, 0, 0)),
                  pl.BlockSpec((1, 2, nc, LANES), lambda s, b: (s, 0, 0, 0))],
        out_specs=pl.BlockSpec((1, 1, nc, LANES), lambda s, b: (s, b, 0, 0)),
        out_shape=jax.ShapeDtypeStruct((two, bg, nc, LANES), BF16),
        compiler_params=_cparams(2, 40),
        name=name,
    )(z, w1, w2, pe, tab)


def _nsa_kernel(q_ref, gate_ref, kc_ref, vct_ref, ks_ref, vst_ref, kw_ref, vwt_ref, ovt_ref, o_ref, kaug_ref,
                *, tq, n_sub, tk, hpg, top_n):
    qi = pl.program_id(2)
    r = hpg * tq
    seq = ks_ref.shape[1]
    nj = ovt_ref.shape[0]
    nc = kc_ref.shape[2]

    @pl.when(qi == 0)
    def _():
        def fill(c, carry):
            r0 = pl.multiple_of(c * tk, tk)
            kaug_ref[pl.ds(r0, tk), 0:LANES] = ks_ref[0, pl.ds(r0, tk), :]
            blk = (r0 + lax.broadcasted_iota(jnp.int32, (tk, nj), 0)) // SLC_LEN
            hot = jnp.where(blk == lax.broadcasted_iota(jnp.int32, (tk, nj), 1), 1.0, 0.0)
            kaug_ref[pl.ds(r0, tk), LANES:LANES + nj] = hot.astype(BF16)
            return carry
        lax.fori_loop(0, seq // tk, fill, 0)

    def select(u):
        t0 = (qi * n_sub + u) * tq
        q = q_ref[0, u * tq:(u + 1) * tq, :]
        qs = jnp.concatenate([q[:, h * LANES:(h + 1) * LANES] for h in range(hpg)], axis=0)
        tcol = t0 + (lax.broadcasted_iota(jnp.int32, (1, r), 1) & (tq - 1))

        s_c = _nt_dot(kc_ref[0, 0], qs)
        cend = lax.broadcasted_iota(jnp.int32, (nc, 1), 0) * CMP_STRIDE + (CMP_LEN - 1)
        ok_c = cend <= tcol
        s_c = jnp.where(ok_c, s_c, NEG_INF)
        e_c = jnp.where(ok_c, jnp.exp2(s_c - jnp.max(s_c, axis=0, keepdims=True)), 0.0)
        p_c = e_c * (1.0 / jnp.maximum(jnp.sum(e_c, axis=0, keepdims=True), TINY))
        o_cmp = _dot(vct_ref[0, 0], p_c.astype(BF16))

        psum = p_c[:, 0:tq]
        for h in range(1, hpg):
            psum = psum + p_c[:, h * tq:(h + 1) * tq]
        ovt = ovt_ref[...]
        p_hi = psum.astype(BF16)
        rem = psum - p_hi.astype(F32)
        p_mid = rem.astype(BF16)
        p_lo = (rem - p_mid.astype(F32)).astype(BF16)
        imp = _dot(ovt, p_hi) + _dot(ovt, p_mid) + _dot(ovt, p_lo)

        jrow = lax.broadcasted_iota(jnp.int32, (nj, tq), 0)
        tok = t0 + lax.broadcasted_iota(jnp.int32, (nj, tq), 1)
        cur = tok // SLC_LEN
        forced = (jrow == 0) | (jrow == cur) | (jrow == cur - 1)
        score = jnp.where(forced, FORCE_SCORE, jnp.where(jrow * SLC_LEN <= tok, imp, NEG_INF))
        bias_t = jnp.full((nj, tq), NEG_INF, F32)
        for _ in range(top_n):
            mx = jnp.max(score, axis=0, keepdims=True)
            first = jnp.min(jnp.where(score == mx, jrow, nj), axis=0, keepdims=True)
            pick = jrow == first
            bias_t = jnp.where(pick, 0.0, bias_t)
            score = jnp.where(pick, REMOVED_SCORE, score)
        bias = bias_t.T.astype(BF16)
        q_aug = jnp.concatenate([qs, jnp.concatenate([bias] * hpg, axis=0)], axis=1)
        return qs, tcol, o_cmp, q_aug

    subs = [select(u) for u in range(n_sub)]

    def slc_tile(kj, carry, diagonal):
        k0 = pl.multiple_of(kj * tk, tk)
        ka = kaug_ref[pl.ds(k0, tk), :]
        vt = vst_ref[0, 0, kj]
        out = []
        scores = [_nt_dot(ka, subs[u][3]) for u in range(n_sub)]
        for u in range(n_sub):
            m_i, acc = carry[2 * u], carry[2 * u + 1]
            s = scores[u]
            if diagonal:
                s = jnp.where(k0 + lax.broadcasted_iota(jnp.int32, (tk, 1), 0) <= subs[u][1], s, NEG_INF)
            m_new = jnp.maximum(m_i, jnp.max(s, axis=0, keepdims=True))
            p = jnp.exp2(s - m_new).astype(BF16)
            out += [m_new, jnp.exp2(m_i - m_new) * acc + _dot(vt, p)]
        return tuple(out)

    n_full = (qi * n_sub * tq) // tk
    carry = (jnp.full((1, r), NEG_INF, F32), jnp.zeros((LANES + SUBLANES, r), F32)) * n_sub
    carry = lax.fori_loop(0, n_full, lambda kj, c: slc_tile(kj, c, False), carry)
    carry = slc_tile(n_full, carry, True)

    n_wt = WIN_LEN // tq + 1
    for u in range(n_sub):
        qs, tcol, o_cmp, _ = subs[u]
        acc_s = carry[2 * u + 1]
        t0 = (qi * n_sub + u) * tq

        start = pl.multiple_of(jnp.maximum(t0 - WIN_LEN, 0), tq)
        s_w = _nt_dot(kw_ref[0, pl.ds(start, n_wt * tq), :], qs)
        dist = tcol - (start + lax.broadcasted_iota(jnp.int32, (n_wt * tq, 1), 0))
        s_w = jnp.where((dist >= 0) & (dist < WIN_LEN), s_w, NEG_INF)
        p_w = jnp.exp2(s_w - jnp.max(s_w, axis=0, keepdims=True)).astype(BF16)
        wt0 = start // tq
        acc_w = _dot(vwt_ref[0, 0, wt0], p_w[0:tq])
        for i in range(1, n_wt):
            acc_w = acc_w + _dot(vwt_ref[0, 0, wt0 + i], p_w[i * tq:(i + 1) * tq])

        g_t = gate_ref[0, u * tq:(u + 1) * tq, :].T

        def gate_row(br):
            return jnp.concatenate([g_t[br * hpg + h:br * hpg + h + 1, :] for h in range(hpg)], axis=1)

        o_t = (gate_row(0) * o_cmp
               + (gate_row(1) * (1.0 / acc_s[LANES:LANES + 1])) * acc_s[0:LANES]
               + (gate_row(2) * (1.0 / acc_w[LANES:LANES + 1])) * acc_w[0:LANES])
        for h in range(hpg):
            o_ref[0, u * tq:(u + 1) * tq, h * LANES:(h + 1) * LANES] = (
                o_t[:, h * tq:(h + 1) * tq].T.astype(o_ref.dtype))


def _nsa(q, gates, kv, kc, vct, vst, vwt, overlap_t, *, n_groups, hpg, top_n, tq, n_sub, tk, name):
    b, s, _ = q.shape
    nc = kc.shape[2]
    nj = overlap_t.shape[0]
    g_ = n_groups
    seg = lambda sidx: (lambda bi, gi, qi: (bi, 0, sidx * g_ + gi))
    bg = lambda bi, gi, qi: (bi * g_ + gi, 0, 0, 0)
    vrows = LANES + SUBLANES
    tqs = n_sub * tq
    assert tk % tqs == 0 and s % tk == 0
    return pl.pallas_call(
        functools.partial(_nsa_kernel, tq=tq, n_sub=n_sub, tk=tk, hpg=hpg, top_n=top_n),
        grid=(b, g_, s // tqs),
        in_specs=[pl.BlockSpec((1, tqs, hpg * LANES), lambda bi, gi, qi: (bi, qi, gi)),
                  pl.BlockSpec((1, tqs, LANES), lambda bi, gi, qi: (bi, qi, gi)),
                  pl.BlockSpec((1, 1, nc, LANES), bg),
                  pl.BlockSpec((1, 1, LANES, nc), bg),
                  pl.BlockSpec((1, s, LANES), seg(2)),
                  pl.BlockSpec((1, 1, s // tk, vrows, tk), lambda bi, gi, qi: (bi, gi, 0, 0, 0)),
                  pl.BlockSpec((1, s, LANES), seg(4)),
                  pl.BlockSpec((1, 1, s // tq, vrows, tq), lambda bi, gi, qi: (bi, gi, 0, 0, 0)),
                  pl.BlockSpec((nj, nc), lambda bi, gi, qi: (0, 0))],
        out_specs=pl.BlockSpec((1, tqs, hpg * LANES), lambda bi, gi, qi: (bi, qi, gi)),
        out_shape=jax.ShapeDtypeStruct(q.shape, BF16),
        scratch_shapes=[pltpu.VMEM((s, LANES + nj), BF16)],
        compiler_params=_cparams(3, 48),
        name=name,
    )(q, gates, kc, vct, kv, vst, kv, vwt, overlap_t)


NSA_TQ = 128
NSA_SUB = 2
NSA_TK = 512


def _dilated_layer(h, g_attn, w_in, w_out, batch, seq):
    m, d = h.shape
    ng = len(DILATED_GROUPS)
    hg = w_out.shape[0] // HEAD_DIM
    wd = hg * HEAD_DIM
    reach = DILATED_GROUPS[0][0] // DILATED_GROUPS[0][1]
    assert all(win // dl == reach for win, dl in DILATED_GROUPS)
    w = w_in.reshape(d, 3, ng, wd)
    w = jnp.stack([_perm_head_cols(w[:, 0].reshape(d, ng * wd)).reshape(d, ng, wd),
                   _perm_head_cols(w[:, 1].reshape(d, ng * wd)).reshape(d, ng, wd),
                   w[:, 2]], axis=2).reshape(d, ng * 3 * wd).astype(BF16)
    pos = jnp.arange(seq)
    tables = jnp.stack([_rope_table(pos, HEAD_DIM ** -0.5 * LOG2E), _rope_table(pos, 1.0)])
    tn = 512
    per_part = wd // tn
    part = lambda j: (j % (3 * per_part)) // per_part
    qkv = _norm_matmul(h, g_attn, w, out_dtype=BF16, tables=tables, seq=seq,
                       tab_fn=lambda j: jnp.minimum(part(j), 1), rope_fn=lambda j: part(j) < 2,
                       tn=tn, name="a_qkv")
    os_, lses = [], []
    for gi, (_, dil) in enumerate(DILATED_GROUPS):
        t = min(256, seq // dil)
        o, lse = _dil_attn(qkv, batch=batch, seq=seq, group=gi, dil=dil, n_groups=ng, reach=reach,
                           n_heads=hg, t=t, name=f"a_attn{gi}")
        os_.append(o)
        lses.append(lse)
    return _mix_out(os_, lses, w_out.astype(BF16), h, n_heads=hg, name="a_out")


def _value_tiles_t(v, batch, seq, g_, tile):
    vt = v.reshape(batch, seq // tile, tile, g_, HEAD_DIM).transpose(0, 3, 1, 4, 2)
    ones = jnp.ones((batch, g_, seq // tile, SUBLANES, tile), BF16)
    return jnp.concatenate([vt, ones], axis=3)


def _nsa_shared_kv(h, kv_norm_g, w_kv, cmp_k, cmp_v, batch, seq):
    g_ = NSA_KV_GROUPS
    pos = jnp.arange(seq)
    seg_w = g_ * HEAD_DIM
    (pe_k, w1_k, w2_k), (pe_v, w1_v, w2_v) = cmp_k, cmp_v
    w = jnp.concatenate([_perm_head_cols(w_kv[:, s * seg_w:(s + 1) * seg_w]) if s in (2, 4)
                         else w_kv[:, s * seg_w:(s + 1) * seg_w] for s in range(6)], axis=1).astype(BF16)
    kv_tables = jnp.stack([_rope_table(pos, 1.0)])
    kv = _norm_matmul(h, kv_norm_g, w, out_dtype=BF16, tables=kv_tables, seq=seq,
                      tab_fn=lambda j: 0, rope_fn=lambda j: (j == 2) | (j == 4), tn=seg_w, name="b_kv")
    n_chunk = seq // CMP_STRIDE
    n_cmp = (seq - CMP_LEN) // CMP_STRIDE + 1
    assert CMP_LEN == 2 * CMP_STRIDE

    def blocks(seg_idx):
        c = kv[:, seg_idx * seg_w:(seg_idx + 1) * seg_w].reshape(batch, n_chunk, CMP_STRIDE, g_, HEAD_DIM)
        c = c.transpose(0, 3, 1, 2, 4).reshape(batch * g_, n_chunk, CMP_STRIDE * HEAD_DIM)
        return jnp.concatenate([c, jnp.roll(c, -1, axis=1)], axis=-1)

    z = jnp.stack([blocks(0), blocks(1)])
    pe = jnp.stack([pe_k.reshape(1, -1), pe_v.reshape(1, -1)])
    pe = jnp.broadcast_to(pe, (2, SUBLANES, pe.shape[-1]))
    cmp_tab = jnp.stack([_rope_table(jnp.arange(n_chunk) * CMP_STRIDE + CMP_LEN - 1, 1.0),
                         _rope_table(jnp.arange(n_chunk), 1.0, identity=True)])
    cmp_kv = _compress(z, jnp.stack([w1_k, w1_v]).astype(BF16),
                       jnp.stack([_perm_head_cols(w2_k), w2_v]).astype(BF16),
                       pe, cmp_tab, n_cmp=n_cmp, name="b_compress")
    kc = cmp_kv[0][:, None]
    vct = cmp_kv[1].transpose(0, 2, 1)[:, None]
    vst = _value_tiles_t(kv[:, 3 * seg_w:4 * seg_w], batch, seq, g_, min(NSA_TK, seq))
    vwt = _value_tiles_t(kv[:, 5 * seg_w:6 * seg_w], batch, seq, g_, NSA_TQ)
    return kv, kc, vct, vst, vwt


def _nsa_layer(h, g_attn, w_qg, w_out, shared, batch, seq):
    m, d = h.shape
    g_ = NSA_KV_GROUPS
    n_heads = w_out.shape[0] // HEAD_DIM
    hpg = n_heads // g_
    seg_w = g_ * HEAD_DIM
    pos = jnp.arange(seq)
    kv, kc, vct, vst, vwt = shared
    n_chunk = seq // CMP_STRIDE
    n_cmp = (seq - CMP_LEN) // CMP_STRIDE + 1
    n_q = n_heads * HEAD_DIM
    q_tables = jnp.stack([_rope_table(pos, HEAD_DIM ** -0.5 * LOG2E)])
    q = _norm_matmul(h, g_attn, _perm_head_cols(w_qg[:, :n_q]).astype(BF16), out_dtype=BF16,
                     tables=q_tables, seq=seq, tab_fn=lambda j: 0, rope_fn=lambda j: j >= 0, tn=512, name="b_q")
    wg = w_qg[:, n_q:].reshape(d, N_NSA_BRANCH, g_, hpg).transpose(0, 2, 1, 3).reshape(d, g_, N_NSA_BRANCH * hpg)
    wg = jnp.pad(wg, ((0, 0), (0, 0), (0, LANES - N_NSA_BRANCH * hpg))).reshape(d, g_ * LANES)
    gates = _norm_matmul(h, g_attn, wg.astype(BF16), out_dtype=F32, act="sigmoid", tn=g_ * LANES,
                         name="b_gates")

    n_slc = seq // SLC_LEN
    nj = -(-n_slc // LANES) * LANES
    cs = np.arange(n_chunk)[:, None] * CMP_STRIDE
    ss = np.arange(nj)[None, :] * SLC_LEN
    ov = np.clip(np.minimum(cs + CMP_LEN, ss + SLC_LEN) - np.maximum(cs, ss), 0, None) / CMP_LEN
    ov[n_cmp:, :] = 0.0
    overlap_t = jnp.asarray(ov.T, BF16)

    o = _nsa(q.reshape(batch, seq, n_q), gates.reshape(batch, seq, g_ * LANES),
             kv.reshape(batch, seq, 6 * seg_w), kc, vct, vst, vwt, overlap_t,
             n_groups=g_, hpg=hpg, top_n=min(SLC_TOP_N, n_slc), tq=NSA_TQ, n_sub=NSA_SUB, tk=min(NSA_TK, seq),
             name="b_nsa")
    return _matmul_resid(o.reshape(m, n_q), w_out.astype(BF16), h, name="b_out")


def kernel(x, p, a_w_in, a_w_out, b_w_qg, b_w_out, kv_norm_g, w_kv_shared, cmp_pe_k, cmp_w1_k, cmp_w2_k,
           cmp_pe_v, cmp_w1_v, cmp_w2_v, attn_norm_g, mlp_norm_g, mlp_w1, mlp_w2, ple_norm_g, ple_w_gate,
           ple_w_proj, final_norm_g):
    batch, seq, d = x.shape
    m = batch * seq
    depth = attn_norm_g.shape[0]
    n_a = a_w_in.shape[0]
    h = x.reshape(m, d)
    shared = None
    for i in range(depth):
        if i < n_a:
            h = _dilated_layer(h, attn_norm_g[i], a_w_in[i], a_w_out[i], batch, seq)
        else:
            if i == n_a:
                shared = _nsa_shared_kv(h, kv_norm_g, w_kv_shared, (cmp_pe_k, cmp_w1_k, cmp_w2_k),
                                        (cmp_pe_v, cmp_w1_v, cmp_w2_v), batch, seq)
            j = i - n_a
            h = _nsa_layer(h, attn_norm_g[i], b_w_qg[j], b_w_out[j], shared, batch, seq)
        h = _mlp(h, mlp_norm_g[i], mlp_w1[i].astype(BF16), mlp_w2[i].astype(BF16), name=f"mlp{i}")
        h = _ple(h, ple_norm_g[i], ple_w_gate[i].astype(BF16), p[i].reshape(m, -1),
                 ple_w_proj[i].astype(BF16), name=f"ple{i}")
    return _rmsnorm(h, final_norm_g, name="final_norm").reshape(batch, seq, d)
```

```python
import functools
import math

import jax
import jax.numpy as jnp
import numpy as np
from jax import lax
from jax.experimental import pallas as pl
from jax.experimental.pallas import tpu as pltpu

F32 = jnp.float32
BF16 = jnp.bfloat16

HEAD_DIM = 128
ROT_DIM = HEAD_DIM // 4
ROPE_THETA = 500000.0
NORM_EPS = 1e-6
NEG_INF = -1e30
FORCE_SCORE = 1e9
TINY = 1e-20
REMOVED_SCORE = -3.0e38
LOG2E = math.log2(math.e)

DILATED_GROUPS = ((128, 1), (512, 4), (2048, 16))
NSA_KV_GROUPS = 4
CMP_LEN = 32
CMP_STRIDE = 16
SLC_LEN = 64
SLC_TOP_N = 16
WIN_LEN = 512
N_NSA_BRANCH = 3

LANES = 128
SUBLANES = 8
MIB = 1024 * 1024


def _cparams(n_grid, vmem_mib):
    return pltpu.CompilerParams(dimension_semantics=("arbitrary",) * n_grid,
                                vmem_limit_bytes=vmem_mib * MIB)


def _nt_dot(a, b):
    return lax.dot_general(a, b, (((1,), (1,)), ((), ())), preferred_element_type=F32)


def _dot(a, b):
    return jnp.dot(a, b, preferred_element_type=F32)


def _perm_head_cols(w):
    k, n = w.shape
    half = ROT_DIM // 2
    w = w.reshape(k, n // HEAD_DIM, HEAD_DIM)
    w = jnp.concatenate([w[..., :half], w[..., ROT_DIM:LANES // 2 + half], w[..., half:ROT_DIM],
                         w[..., LANES // 2 + half:]], axis=-1)
    return w.reshape(k, n)


def _rope_table(pos, scale, identity=False):
    n = pos.shape[0]
    half = ROT_DIM // 2
    if identity:
        return jnp.stack([jnp.full((n, LANES), scale, F32), jnp.zeros((n, LANES), F32)])
    inv = ROPE_THETA ** (-jnp.arange(half, dtype=F32) * (2.0 / ROT_DIM))
    ang = pos.astype(F32)[:, None] * inv[None, :]
    c, s = jnp.cos(ang), jnp.sin(ang)
    ones = jnp.ones((n, LANES // 2 - half), F32)
    zeros = jnp.zeros((n, LANES // 2 - half), F32)
    cos = jnp.concatenate([c, ones, c, ones], axis=1)
    sg = jnp.concatenate([-s, zeros, s, zeros], axis=1)
    return jnp.stack([cos, sg]) * scale


def _apply_rope(a, cos, sg):
    return a * cos + pltpu.roll(a, LANES // 2, 1) * sg


def _rms_scale(x, g):
    ms = jnp.mean(x * x, axis=-1, keepdims=True)
    return (x * lax.rsqrt(ms + NORM_EPS) * g).astype(BF16)


def _norm_matmul_kernel(x_ref, g_ref, w_ref, *rest, rope_fn, act):
    if rope_fn is not None:
        tab_ref, o_ref, xn_ref = rest
    else:
        o_ref, xn_ref = rest
    j = pl.program_id(1)

    @pl.when(j == 0)
    def _():
        xn_ref[...] = _rms_scale(x_ref[...], g_ref[...])

    acc = _dot(xn_ref[...], w_ref[...])
    if rope_fn is None:
        if act == "sigmoid":
            acc = 1.0 / (1.0 + jnp.exp(-acc))
        o_ref[...] = acc.astype(o_ref.dtype)
        return
    is_rope = rope_fn(j)

    @pl.when(is_rope)
    def _():
        cos, sg = tab_ref[0, 0], tab_ref[0, 1]
        for c in range(acc.shape[1] // LANES):
            sl = slice(c * LANES, (c + 1) * LANES)
            o_ref[:, sl] = _apply_rope(acc[:, sl], cos, sg).astype(o_ref.dtype)

    @pl.when(jnp.logical_not(is_rope))
    def _():
        o_ref[...] = acc.astype(o_ref.dtype)


def _norm_matmul(x, g, w, *, out_dtype, tables=None, tab_fn=None, rope_fn=None, seq=None, act=None,
                 tm=512, tn=512, name):
    m, k = x.shape
    n = w.shape[1]
    tm, tn = min(tm, m), min(tn, n)
    in_specs = [pl.BlockSpec((tm, k), lambda i, j: (i, 0)),
                pl.BlockSpec((1, k), lambda i, j: (0, 0)),
                pl.BlockSpec((k, tn), lambda i, j: (0, j))]
    args = [x, g.reshape(1, k), w]
    if tables is not None:
        nsb = seq // tm
        in_specs.append(pl.BlockSpec((1, 2, tm, LANES), lambda i, j: (tab_fn(j), 0, i % nsb, 0)))
        args.append(tables)
    return pl.pallas_call(
        functools.partial(_norm_matmul_kernel, rope_fn=rope_fn, act=act),
        grid=(m // tm, n // tn),
        in_specs=in_specs,
        out_specs=pl.BlockSpec((tm, tn), lambda i, j: (i, j)),
        out_shape=jax.ShapeDtypeStruct((m, n), out_dtype),
        scratch_shapes=[pltpu.VMEM((tm, k), BF16)],
        compiler_params=_cparams(2, 40),
        name=name,
    )(*args)


def _matmul_resid_kernel(a_ref, w_ref, r_ref, o_ref):
    o_ref[...] = r_ref[...] + _dot(a_ref[...], w_ref[...])


def _matmul_resid(a, w, r, *, tm=512, tn=1024, name):
    m, k = a.shape
    n = w.shape[1]
    tm, tn = min(tm, m), min(tn, n)
    return pl.pallas_call(
        _matmul_resid_kernel,
        grid=(m // tm, n // tn),
        in_specs=[pl.BlockSpec((tm, k), lambda i, j: (i, 0)),
                  pl.BlockSpec((k, tn), lambda i, j: (0, j)),
                  pl.BlockSpec((tm, tn), lambda i, j: (i, j))],
        out_specs=pl.BlockSpec((tm, tn), lambda i, j: (i, j)),
        out_shape=jax.ShapeDtypeStruct((m, n), F32),
        compiler_params=_cparams(2, 40),
        name=name,
    )(a, w, r)


def _mlp_kernel(x_ref, g_ref, w1_ref, w2_ref, o_ref, xn_ref, acc_ref):
    f = pl.program_id(1)

    @pl.when(f == 0)
    def _():
        xn_ref[...] = _rms_scale(x_ref[...], g_ref[...])
        acc_ref[...] = jnp.zeros_like(acc_ref)

    a = _dot(xn_ref[...], w1_ref[...])
    a = jnp.square(jnp.maximum(a, 0.0)).astype(BF16)
    acc_ref[...] += _dot(a, w2_ref[...])

    @pl.when(f == pl.num_programs(1) - 1)
    def _():
        o_ref[...] = x_ref[...] + acc_ref[...]


def _mlp(h, g, w1, w2, *, tm=512, tf=1024, name):
    m, d = h.shape
    ff = w1.shape[1]
    tm, tf = min(tm, m), min(tf, ff)
    return pl.pallas_call(
        _mlp_kernel,
        grid=(m // tm, ff // tf),
        in_specs=[pl.BlockSpec((tm, d), lambda i, f: (i, 0)),
                  pl.BlockSpec((1, d), lambda i, f: (0, 0)),
                  pl.BlockSpec((d, tf), lambda i, f: (0, f)),
                  pl.BlockSpec((tf, d), lambda i, f: (f, 0))],
        out_specs=pl.BlockSpec((tm, d), lambda i, f: (i, 0)),
        out_shape=jax.ShapeDtypeStruct((m, d), F32),
        scratch_shapes=[pltpu.VMEM((tm, d), BF16), pltpu.VMEM((tm, d), F32)],
        compiler_params=_cparams(2, 56),
        name=name,
    )(h, g.reshape(1, d), w1, w2)


def _ple_kernel(x_ref, g_ref, wg_ref, p_ref, wp_ref, r_ref, o_ref, xn_ref):
    @pl.when(pl.program_id(1) == 0)
    def _():
        xn_ref[...] = _rms_scale(x_ref[...], g_ref[...])

    z = _dot(xn_ref[...], wg_ref[...])
    gate = 1.0 / (1.0 + jnp.exp(-z))
    pp = _dot(p_ref[...].astype(BF16), wp_ref[...])
    o_ref[...] = r_ref[...] + pp * gate


def _ple(h, g, wg, p, wp, *, tm=512, tn=1024, name):
    m, d = h.shape
    dp = p.shape[1]
    n = wg.shape[1]
    tm, tn = min(tm, m), min(tn, n)
    return pl.pallas_call(
        _ple_kernel,
        grid=(m // tm, n // tn),
        in_specs=[pl.BlockSpec((tm, d), lambda i, j: (i, 0)),
                  pl.BlockSpec((1, d), lambda i, j: (0, 0)),
                  pl.BlockSpec((d, tn), lambda i, j: (0, j)),
                  pl.BlockSpec((tm, dp), lambda i, j: (i, 0)),
                  pl.BlockSpec((dp, tn), lambda i, j: (0, j)),
                  pl.BlockSpec((tm, tn), lambda i, j: (i, j))],
        out_specs=pl.BlockSpec((tm, tn), lambda i, j: (i, j)),
        out_shape=jax.ShapeDtypeStruct((m, n), F32),
        scratch_shapes=[pltpu.VMEM((tm, d), BF16)],
        compiler_params=_cparams(2, 48),
        name=name,
    )(h, g.reshape(1, d), wg, p, wp, h)


def _rmsnorm_kernel(x_ref, g_ref, o_ref):
    x = x_ref[...]
    ms = jnp.mean(x * x, axis=-1, keepdims=True)
    o_ref[...] = x * lax.rsqrt(ms + NORM_EPS) * g_ref[...]


def _rmsnorm(h, g, *, tm=512, name):
    m, d = h.shape
    tm = min(tm, m)
    return pl.pallas_call(
        _rmsnorm_kernel,
        grid=(m // tm,),
        in_specs=[pl.BlockSpec((tm, d), lambda i: (i, 0)),
                  pl.BlockSpec((1, d), lambda i: (0, 0))],
        out_specs=pl.BlockSpec((tm, d), lambda i: (i, 0)),
        out_shape=jax.ShapeDtypeStruct((m, d), F32),
        compiler_params=_cparams(1, 32),
        name=name,
    )(h, g.reshape(1, d))


def _a_qkv_kernel(x_ref, g_ref, w_ref, tab_ref, *rest, dils, blocks_per_group, blocks_per_part):
    n_g = len(dils)
    o_refs = rest[:n_g]
    xn_ref, res_ref = rest[n_g:]
    j = pl.program_id(1)

    @pl.when(j == 0)
    def _():
        xn_ref[...] = _rms_scale(x_ref[...], g_ref[...])

    acc = _dot(xn_ref[...], w_ref[...])
    tm, tn = acc.shape
    is_rope = (j % blocks_per_group) // blocks_per_part < 2

    @pl.when(is_rope)
    def _():
        cos, sg = tab_ref[0, 0], tab_ref[0, 1]
        for c in range(tn // LANES):
            res_ref[c] = _apply_rope(acc[:, c * LANES:(c + 1) * LANES], cos, sg)

    @pl.when(jnp.logical_not(is_rope))
    def _():
        for c in range(tn // LANES):
            res_ref[c] = acc[:, c * LANES:(c + 1) * LANES]

    for gi, d in enumerate(dils):
        @pl.when(j // blocks_per_group == gi)
        def _(gi=gi, d=d):
            for r in range(d):
                for c in range(tn // LANES):
                    o_refs[gi][0, r, :, c * LANES:(c + 1) * LANES] = (
                        res_ref[c, pl.ds(r, tm // d, stride=d), :].astype(BF16))


def _a_qkv(x, g, w, tables, *, batch, seq, dils, group_width, part_width, tm=512, tn=512, name):
    m, k = x.shape
    nsb = seq // tm
    bpg = group_width // tn
    bpp = part_width // tn
    n_g = len(dils)
    assert all(tm % (d * 16) == 0 for d in dils)

    def out_map(gi):
        return lambda i, j: (i // nsb, 0, i % nsb, jnp.clip(j - gi * bpg, 0, bpg - 1))

    return pl.pallas_call(
        functools.partial(_a_qkv_kernel, dils=dils, blocks_per_group=bpg, blocks_per_part=bpp),
        grid=(m // tm, n_g * bpg),
        in_specs=[pl.BlockSpec((tm, k), lambda i, j: (i, 0)),
                  pl.BlockSpec((1, k), lambda i, j: (0, 0)),
                  pl.BlockSpec((k, tn), lambda i, j: (0, j)),
                  pl.BlockSpec((1, 2, tm, LANES),
                               lambda i, j: (jnp.minimum((j % bpg) // bpp, 1), 0, i % nsb, 0))],
        out_specs=[pl.BlockSpec((1, d, tm // d, tn), out_map(gi)) for gi, d in enumerate(dils)],
        out_shape=[jax.ShapeDtypeStruct((batch, d, seq // d, group_width), BF16) for d in dils],
        scratch_shapes=[pltpu.VMEM((tm, k), BF16), pltpu.VMEM((tn // LANES, tm, LANES), F32)],
        compiler_params=_cparams(2, 40),
        name=name,
    )(x, g.reshape(1, k), w, tables)


def _dil_attn_kernel(q_ref, kp_ref, kc_ref, vp_ref, vc_ref, o_ref, lse_ref, *, reach, n_heads):
    i = pl.program_id(2)
    t = q_ref.shape[2]
    tp = kp_ref.shape[2]
    row = lax.broadcasted_iota(jnp.int32, (t, tp + t), 0)
    col = lax.broadcasted_iota(jnp.int32, (t, tp + t), 1)
    no_prev = jnp.where(i > 0, 0, reach + 1 + tp + t)
    dist = row + tp - col + jnp.where(col < tp, no_prev, 0)
    valid = dist.astype(jnp.uint32) <= reach
    lane = lax.broadcasted_iota(jnp.int32, (t, LANES), 1)
    lse_tile = jnp.zeros((t, LANES), F32)
    for h in range(n_heads):
        sl = slice(h * LANES, (h + 1) * LANES)
        k = jnp.concatenate([kp_ref[0, 0, :, sl], kc_ref[0, 0, :, sl]], axis=0)
        v = jnp.concatenate([vp_ref[0, 0, :, sl], vc_ref[0, 0, :, sl]], axis=0)
        s = jnp.where(valid, _nt_dot(q_ref[0, 0, :, sl], k), NEG_INF)
        m = jnp.max(s, axis=-1, keepdims=True)
        p = jnp.exp2(s - m)
        l = jnp.sum(p, axis=-1, keepdims=True)
        o_ref[0, 0, :, sl] = (_dot(p.astype(BF16), v) * (1.0 / l)).astype(o_ref.dtype)
        lse_tile = jnp.where(lane == h, m + jnp.log2(l), lse_tile)
    lse_ref[0, 0] = lse_tile


def _dil_attn(qkv, *, reach, n_heads, t, name):
    batch, dil, su, _ = qkv.shape
    w = n_heads * LANES
    tp = reach
    assert t % tp == 0 and tp % 16 == 0
    cur = lambda part: (lambda b, r, i: (b, r, i, part))
    prev = lambda part: (lambda b, r, i: (b, r, jnp.maximum(i * (t // tp) - 1, 0), part))
    return pl.pallas_call(
        functools.partial(_dil_attn_kernel, reach=reach, n_heads=n_heads),
        grid=(batch, dil, su // t),
        in_specs=[pl.BlockSpec((1, 1, t, w), cur(0)),
                  pl.BlockSpec((1, 1, tp, w), prev(1)),
                  pl.BlockSpec((1, 1, t, w), cur(1)),
                  pl.BlockSpec((1, 1, tp, w), prev(2)),
                  pl.BlockSpec((1, 1, t, w), cur(2))],
        out_specs=[pl.BlockSpec((1, 1, t, w), lambda b, r, i: (b, r, i, 0)),
                   pl.BlockSpec((1, 1, t, LANES), lambda b, r, i: (b, r, i, 0))],
        out_shape=[jax.ShapeDtypeStruct((batch, dil, su, w), BF16),
                   jax.ShapeDtypeStruct((batch, dil, su, LANES), F32)],
        compiler_params=_cparams(3, 40),
        name=name,
    )(qkv, qkv, qkv, qkv, qkv)


def _mix_out_kernel(*refs, dils, n_heads):
    n_g = len(dils)
    o_refs = refs[:n_g]
    lse_refs = refs[n_g:2 * n_g]
    w_ref, r_ref, out_ref, oc_ref, uo_ref, ul_ref = refs[2 * n_g:]
    tm = oc_ref.shape[0]

    @pl.when(pl.program_id(1) == 0)
    def _():
        for g, d in enumerate(dils):
            for r in range(d):
                rows = pl.ds(r, tm // d, stride=d)
                for h in range(n_heads):
                    uo_ref[g, h, rows, :] = o_refs[g][0, r, :, h * LANES:(h + 1) * LANES].astype(F32)
                ul_ref[g, rows, :] = lse_refs[g][0, r]
        m = ul_ref[0]
        for g in range(1, n_g):
            m = jnp.maximum(m, ul_ref[g])
        e = [jnp.exp2(ul_ref[g] - m) for g in range(n_g)]
        den = e[0]
        for g in range(1, n_g):
            den = den + e[g]
        inv = 1.0 / den
        for h in range(n_heads):
            sl = slice(h * LANES, (h + 1) * LANES)
            acc = (e[0] * inv)[:, h:h + 1] * uo_ref[0, h]
            for g in range(1, n_g):
                acc = acc + (e[g] * inv)[:, h:h + 1] * uo_ref[g, h]
            oc_ref[:, sl] = acc.astype(BF16)

    out_ref[...] = r_ref[...] + _dot(oc_ref[...], w_ref[...])


def _mix_out(os_, lses, w, r, *, seq, n_heads, tm=512, tn=1024, name):
    m, n = r.shape
    k = w.shape[0]
    dils = tuple(o.shape[1] for o in os_)
    nsb = seq // tm
    tn = min(tn, n)
    cls = lambda i, j: (i // nsb, 0, i % nsb, 0)
    return pl.pallas_call(
        functools.partial(_mix_out_kernel, dils=dils, n_heads=n_heads),
        grid=(m // tm, n // tn),
        in_specs=([pl.BlockSpec((1, d, tm // d, k), cls) for d in dils]
                  + [pl.BlockSpec((1, d, tm // d, LANES), cls) for d in dils]
                  + [pl.BlockSpec((k, tn), lambda i, j: (0, j)),
                     pl.BlockSpec((tm, tn), lambda i, j: (i, j))]),
        out_specs=pl.BlockSpec((tm, tn), lambda i, j: (i, j)),
        out_shape=jax.ShapeDtypeStruct((m, n), F32),
        scratch_shapes=[pltpu.VMEM((tm, k), BF16), pltpu.VMEM((len(dils), n_heads, tm, LANES), F32),
                        pltpu.VMEM((len(dils), tm, LANES), F32)],
        compiler_params=_cparams(2, 48),
        name=name,
    )(*os_, *lses, w, r)


def _cmp_kernel(z_ref, w1_ref, w2_ref, pe_ref, tab_ref, o_ref, *, n_cmp):
    w1 = w1_ref[0]
    bias = _dot(pe_ref[0].astype(BF16), w1)[0:1]
    hid = _dot(z_ref[0, 0], w1) + bias
    hid = 0.5 * hid * (1.0 + jnp.tanh(math.sqrt(2.0 / math.pi) * (hid + 0.044715 * (hid * hid * hid))))
    o = _apply_rope(_dot(hid.astype(BF16), w2_ref[0]), tab_ref[0, 0], tab_ref[0, 1])
    row = lax.broadcasted_iota(jnp.int32, o.shape, 0)
    o_ref[0, 0] = jnp.where(row < n_cmp, o, 0.0).astype(o_ref.dtype)


def _compress(z, w1, w2, pe, tab, *, n_cmp, name):
    two, bg, nc, ld = z.shape
    hid = w1.shape[2]
    return pl.pallas_call(
        functools.partial(_cmp_kernel, n_cmp=n_cmp),
        grid=(two, bg),
        in_specs=[pl.BlockSpec((1, 1, nc, ld), lambda s, b: (s, b, 0, 0)),
                  pl.BlockSpec((1, ld, hid), lambda s, b: (s, 0, 0)),
                  pl.BlockSpec((1, hid, LANES), lambda s, b: (s, 0, 0)),
                  pl.BlockSpec((1, SUBLANES, ld), lambda s, b: (s, 0, 0)),
                  pl.BlockSpec((1, 2, nc, LANES), lambda s, b: (s, 0, 0, 0))],
        out_specs=pl.BlockSpec((1, 1, nc, LANES), lambda s, b: (s, b, 0, 0)),
        out_shape=jax.ShapeDtypeStruct((two, bg, nc, LANES), BF16),
        compiler_params=_cparams(2, 40),
        name=name,
    )(z, w1, w2, pe, tab)


def _nsa_kernel(q_ref, gate_ref, kc_ref, vct_ref, ks_ref, vst_ref, kw_ref, vwt_ref, ovt_ref, o_ref, kaug_ref,
                *, tq, n_sub, tk, hpg, top_n):
    qi = pl.program_id(2)
    r = hpg * tq
    seq = ks_ref.shape[1]
    nj = ovt_ref.shape[0]
    nc = kc_ref.shape[2]

    @pl.when(qi == 0)
    def _():
        def fill(c, carry):
            r0 = pl.multiple_of(c * tk, tk)
            kaug_ref[pl.ds(r0, tk), 0:LANES] = ks_ref[0, pl.ds(r0, tk), :]
            blk = (r0 + lax.broadcasted_iota(jnp.int32, (tk, nj), 0)) // SLC_LEN
            hot = jnp.where(blk == lax.broadcasted_iota(jnp.int32, (tk, nj), 1), 1.0, 0.0)
            kaug_ref[pl.ds(r0, tk), LANES:LANES + nj] = hot.astype(BF16)
            return carry
        lax.fori_loop(0, seq // tk, fill, 0)

    subs = range(n_sub)
    n_wt = WIN_LEN // tq + 1
    span = n_wt * tq
    t0 = [(qi * n_sub + u) * tq for u in subs]
    qs, tcol = [], []
    for u in subs:
        q = q_ref[0, u * tq:(u + 1) * tq, :]
        qs.append(jnp.concatenate([q[:, h * LANES:(h + 1) * LANES] for h in range(hpg)], axis=0))
        tcol.append(t0[u] + (lax.broadcasted_iota(jnp.int32, (1, r), 1) & (tq - 1)))

    s_c = [_nt_dot(kc_ref[0, 0], qs[u]) for u in subs]
    start = [pl.multiple_of(jnp.maximum(t0[u] - WIN_LEN, 0), tq) for u in subs]
    s_w = [_nt_dot(kw_ref[0, pl.ds(start[u], span), :], qs[u]) for u in subs]

    cend = lax.broadcasted_iota(jnp.int32, (nc, 1), 0) * CMP_STRIDE + (CMP_LEN - 1)
    ovt = ovt_ref[...]
    o_cmp, imp = [], []
    for u in subs:
        ok_c = cend <= tcol[u]
        sc = jnp.where(ok_c, s_c[u], NEG_INF)
        e_c = jnp.where(ok_c, jnp.exp2(sc - jnp.max(sc, axis=0, keepdims=True)), 0.0)
        p_c = e_c * (1.0 / jnp.maximum(jnp.sum(e_c, axis=0, keepdims=True), TINY))
        o_cmp.append(_dot(vct_ref[0, 0], p_c.astype(BF16)))
        psum = p_c[:, 0:tq]
        for h in range(1, hpg):
            psum = psum + p_c[:, h * tq:(h + 1) * tq]
        p_hi = psum.astype(BF16)
        rem = psum - p_hi.astype(F32)
        p_mid = rem.astype(BF16)
        p_lo = (rem - p_mid.astype(F32)).astype(BF16)
        imp.append(_dot(ovt, p_hi) + _dot(ovt, p_mid) + _dot(ovt, p_lo))

    acc_w = []
    for u in subs:
        dist = tcol[u] - (start[u] + lax.broadcasted_iota(jnp.int32, (span, 1), 0))
        sw = jnp.where(dist.astype(jnp.uint32) < WIN_LEN, s_w[u], NEG_INF)
        p_w = jnp.exp2(sw - jnp.max(sw, axis=0, keepdims=True)).astype(BF16)
        wt0 = start[u] // tq
        vw = jnp.concatenate([vwt_ref[0, 0, wt0 + i] for i in range(n_wt)], axis=1)
        acc_w.append(_dot(vw, p_w))

    jrow = lax.broadcasted_iota(jnp.int32, (nj, tq), 0)
    score, bias_t = [], []
    for u in subs:
        tok = t0[u] + lax.broadcasted_iota(jnp.int32, (nj, tq), 1)
        cur = tok // SLC_LEN
        forced = (jrow == 0) | (jrow == cur) | (jrow == cur - 1)
        score.append(jnp.where(forced, FORCE_SCORE, jnp.where(jrow * SLC_LEN <= tok, imp[u], NEG_INF)))
        bias_t.append(jnp.full((nj, tq), NEG_INF, F32))
    for _ in range(top_n):
        for u in subs:
            mx = jnp.max(score[u], axis=0, keepdims=True)
            first = jnp.min(jnp.where(score[u] == mx, jrow, nj), axis=0, keepdims=True)
            pick = jrow == first
            bias_t[u] = jnp.where(pick, 0.0, bias_t[u])
            score[u] = jnp.where(pick, REMOVED_SCORE, score[u])
    q_aug = []
    for u in subs:
        bias = bias_t[u].T.astype(BF16)
        q_aug.append(jnp.concatenate([qs[u], jnp.concatenate([bias] * hpg, axis=0)], axis=1))

    def slc_tile(kj, carry, diagonal):
        k0 = pl.multiple_of(kj * tk, tk)
        ka = kaug_ref[pl.ds(k0, tk), :]
        vt = vst_ref[0, 0, kj]
        out = []
        scores = [_nt_dot(ka, q_aug[u]) for u in subs]
        for u in subs:
            m_i, acc = carry[2 * u], carry[2 * u + 1]
            s = scores[u]
            if diagonal:
                s = jnp.where(k0 + lax.broadcasted_iota(jnp.int32, (tk, 1), 0) <= tcol[u], s, NEG_INF)
            m_new = jnp.maximum(m_i, jnp.max(s, axis=0, keepdims=True))
            p = jnp.exp2(s - m_new).astype(BF16)
            out += [m_new, jnp.exp2(m_i - m_new) * acc + _dot(vt, p)]
        return tuple(out)

    n_full = (qi * n_sub * tq) // tk
    carry = (jnp.full((1, r), NEG_INF, F32), jnp.zeros((LANES + SUBLANES, r), F32)) * n_sub
    carry = lax.fori_loop(0, n_full, lambda kj, c: slc_tile(kj, c, False), carry)
    carry = slc_tile(n_full, carry, True)

    for u in subs:
        acc_s = carry[2 * u + 1]
        g_t = gate_ref[0, u * tq:(u + 1) * tq, :].T

        def gate_row(br):
            return jnp.concatenate([g_t[br * hpg + h:br * hpg + h + 1, :] for h in range(hpg)], axis=1)

        o_t = (gate_row(0) * o_cmp[u]
               + (gate_row(1) * (1.0 / acc_s[LANES:LANES + 1])) * acc_s[0:LANES]
               + (gate_row(2) * (1.0 / acc_w[u][LANES:LANES + 1])) * acc_w[u][0:LANES])
        for h in range(hpg):
            o_ref[0, u * tq:(u + 1) * tq, h * LANES:(h + 1) * LANES] = (
                o_t[:, h * tq:(h + 1) * tq].T.astype(o_ref.dtype))


def _nsa(q, gates, kv, kc, vct, vst, vwt, overlap_t, *, n_groups, hpg, top_n, tq, n_sub, tk, name):
    b, s, _ = q.shape
    nc = kc.shape[2]
    nj = overlap_t.shape[0]
    g_ = n_groups
    seg = lambda sidx: (lambda bi, gi, qi: (bi, 0, sidx * g_ + gi))
    bg = lambda bi, gi, qi: (bi * g_ + gi, 0, 0, 0)
    vrows = LANES + SUBLANES
    tqs = n_sub * tq
    assert tk % tqs == 0 and s % tk == 0
    return pl.pallas_call(
        functools.partial(_nsa_kernel, tq=tq, n_sub=n_sub, tk=tk, hpg=hpg, top_n=top_n),
        grid=(b, g_, s // tqs),
        in_specs=[pl.BlockSpec((1, tqs, hpg * LANES), lambda bi, gi, qi: (bi, qi, gi)),
                  pl.BlockSpec((1, tqs, LANES), lambda bi, gi, qi: (bi, qi, gi)),
                  pl.BlockSpec((1, 1, nc, LANES), bg),
                  pl.BlockSpec((1, 1, LANES, nc), bg),
                  pl.BlockSpec((1, s, LANES), seg(2)),
                  pl.BlockSpec((1, 1, s // tk, vrows, tk), lambda bi, gi, qi: (bi, gi, 0, 0, 0)),
                  pl.BlockSpec((1, s, LANES), seg(4)),
                  pl.BlockSpec((1, 1, s // tq, vrows, tq), lambda bi, gi, qi: (bi, gi, 0, 0, 0)),
                  pl.BlockSpec((nj, nc), lambda bi, gi, qi: (0, 0))],
        out_specs=pl.BlockSpec((1, tqs, hpg * LANES), lambda bi, gi, qi: (bi, qi, gi)),
        out_shape=jax.ShapeDtypeStruct(q.shape, BF16),
        scratch_shapes=[pltpu.VMEM((s, LANES + nj), BF16)],
        compiler_params=_cparams(3, 48),
        name=name,
    )(q, gates, kc, vct, kv, vst, kv, vwt, overlap_t)


NSA_TQ = 128
NSA_SUB = 4
NSA_TK = 512


def _dilated_layer(h, g_attn, w_in, w_out, batch, seq):
    m, d = h.shape
    ng = len(DILATED_GROUPS)
    dils = tuple(dl for _, dl in DILATED_GROUPS)
    hg = w_out.shape[0] // HEAD_DIM
    wd = hg * HEAD_DIM
    reach = DILATED_GROUPS[0][0] // DILATED_GROUPS[0][1]
    assert all(win // dl == reach for win, dl in DILATED_GROUPS)
    w = w_in.reshape(d, 3, ng, wd)
    w = jnp.stack([_perm_head_cols(w[:, 0].reshape(d, ng * wd)).reshape(d, ng, wd),
                   _perm_head_cols(w[:, 1].reshape(d, ng * wd)).reshape(d, ng, wd),
                   w[:, 2]], axis=2).reshape(d, ng * 3 * wd).astype(BF16)
    pos = jnp.arange(seq)
    tables = jnp.stack([_rope_table(pos, HEAD_DIM ** -0.5 * LOG2E), _rope_table(pos, 1.0)])
    tm = min(512, seq)
    qkvs = _a_qkv(h, g_attn, w, tables, batch=batch, seq=seq, dils=dils, group_width=3 * wd, part_width=wd,
                  tm=tm, name="a_qkv")
    os_, lses = [], []
    for gi, dil in enumerate(dils):
        o, lse = _dil_attn(qkvs[gi], reach=reach, n_heads=hg, t=min(256, seq // dil), name=f"a_attn{gi}")
        os_.append(o)
        lses.append(lse)
    return _mix_out(os_, lses, w_out.astype(BF16), h, seq=seq, n_heads=hg, tm=tm, name="a_out")


def _value_tiles_t(v, batch, seq, g_, tile):
    vt = v.reshape(batch, seq // tile, tile, g_, HEAD_DIM).transpose(0, 3, 1, 4, 2)
    ones = jnp.ones((batch, g_, seq // tile, SUBLANES, tile), BF16)
    return jnp.concatenate([vt, ones], axis=3)


def _nsa_shared_kv(h, kv_norm_g, w_kv, cmp_k, cmp_v, batch, seq):
    g_ = NSA_KV_GROUPS
    pos = jnp.arange(seq)
    seg_w = g_ * HEAD_DIM
    (pe_k, w1_k, w2_k), (pe_v, w1_v, w2_v) = cmp_k, cmp_v
    w = jnp.concatenate([_perm_head_cols(w_kv[:, s * seg_w:(s + 1) * seg_w]) if s in (2, 4)
                         else w_kv[:, s * seg_w:(s + 1) * seg_w] for s in range(6)], axis=1).astype(BF16)
    kv_tables = jnp.stack([_rope_table(pos, 1.0)])
    kv = _norm_matmul(h, kv_norm_g, w, out_dtype=BF16, tables=kv_tables, seq=seq,
                      tab_fn=lambda j: 0, rope_fn=lambda j: (j == 2) | (j == 4), tn=seg_w, name="b_kv")
    n_chunk = seq // CMP_STRIDE
    n_cmp = (seq - CMP_LEN) // CMP_STRIDE + 1
    assert CMP_LEN == 2 * CMP_STRIDE

    def blocks(seg_idx):
        c = kv[:, seg_idx * seg_w:(seg_idx + 1) * seg_w].reshape(batch, n_chunk, CMP_STRIDE, g_, HEAD_DIM)
        c = c.transpose(0, 3, 1, 2, 4).reshape(batch * g_, n_chunk, CMP_STRIDE * HEAD_DIM)
        return jnp.concatenate([c, jnp.roll(c, -1, axis=1)], axis=-1)

    z = jnp.stack([blocks(0), blocks(1)])
    pe = jnp.stack([pe_k.reshape(1, -1), pe_v.reshape(1, -1)])
    pe = jnp.broadcast_to(pe, (2, SUBLANES, pe.shape[-1]))
    cmp_tab = jnp.stack([_rope_table(jnp.arange(n_chunk) * CMP_STRIDE + CMP_LEN - 1, 1.0),
                         _rope_table(jnp.arange(n_chunk), 1.0, identity=True)])
    cmp_kv = _compress(z, jnp.stack([w1_k, w1_v]).astype(BF16),
                       jnp.stack([_perm_head_cols(w2_k), w2_v]).astype(BF16),
                       pe, cmp_tab, n_cmp=n_cmp, name="b_compress")
    kc = cmp_kv[0][:, None]
    vct = cmp_kv[1].transpose(0, 2, 1)[:, None]
    vst = _value_tiles_t(kv[:, 3 * seg_w:4 * seg_w], batch, seq, g_, min(NSA_TK, seq))
    vwt = _value_tiles_t(kv[:, 5 * seg_w:6 * seg_w], batch, seq, g_, NSA_TQ)
    return kv, kc, vct, vst, vwt


def _nsa_layer(h, g_attn, w_qg, w_out, shared, batch, seq):
    m, d = h.shape
    g_ = NSA_KV_GROUPS
    n_heads = w_out.shape[0] // HEAD_DIM
    hpg = n_heads // g_
    seg_w = g_ * HEAD_DIM
    pos = jnp.arange(seq)
    kv, kc, vct, vst, vwt = shared
    n_chunk = seq // CMP_STRIDE
    n_cmp = (seq - CMP_LEN) // CMP_STRIDE + 1
    n_q = n_heads * HEAD_DIM
    q_tables = jnp.stack([_rope_table(pos, HEAD_DIM ** -0.5 * LOG2E)])
    q = _norm_matmul(h, g_attn, _perm_head_cols(w_qg[:, :n_q]).astype(BF16), out_dtype=BF16,
                     tables=q_tables, seq=seq, tab_fn=lambda j: 0, rope_fn=lambda j: j >= 0, tn=512, name="b_q")
    wg = w_qg[:, n_q:].reshape(d, N_NSA_BRANCH, g_, hpg).transpose(0, 2, 1, 3).reshape(d, g_, N_NSA_BRANCH * hpg)
    wg = jnp.pad(wg, ((0, 0), (0, 0), (0, LANES - N_NSA_BRANCH * hpg))).reshape(d, g_ * LANES)
    gates = _norm_matmul(h, g_attn, wg.astype(BF16), out_dtype=F32, act="sigmoid", tn=g_ * LANES,
                         name="b_gates")

    n_slc = seq // SLC_LEN
    nj = -(-n_slc // LANES) * LANES
    cs = np.arange(n_chunk)[:, None] * CMP_STRIDE
    ss = np.arange(nj)[None, :] * SLC_LEN
    ov = np.clip(np.minimum(cs + CMP_LEN, ss + SLC_LEN) - np.maximum(cs, ss), 0, None) / CMP_LEN
    ov[n_cmp:, :] = 0.0
    overlap_t = jnp.asarray(ov.T, BF16)

    o = _nsa(q.reshape(batch, seq, n_q), gates.reshape(batch, seq, g_ * LANES),
             kv.reshape(batch, seq, 6 * seg_w), kc, vct, vst, vwt, overlap_t,
             n_groups=g_, hpg=hpg, top_n=min(SLC_TOP_N, n_slc), tq=NSA_TQ, n_sub=NSA_SUB, tk=min(NSA_TK, seq),
             name="b_nsa")
    return _matmul_resid(o.reshape(m, n_q), w_out.astype(BF16), h, name="b_out")


def kernel(x, p, a_w_in, a_w_out, b_w_qg, b_w_out, kv_norm_g, w_kv_shared, cmp_pe_k, cmp_w1_k, cmp_w2_k,
           cmp_pe_v, cmp_w1_v, cmp_w2_v, attn_norm_g, mlp_norm_g, mlp_w1, mlp_w2, ple_norm_g, ple_w_gate,
           ple_w_proj, final_norm_g):
    batch, seq, d = x.shape
    m = batch * seq
    depth = attn_norm_g.shape[0]
    n_a = a_w_in.shape[0]
    h = x.reshape(m, d)
    shared = None
    for i in range(depth):
        if i < n_a:
            h = _dilated_layer(h, attn_norm_g[i], a_w_in[i], a_w_out[i], batch, seq)
        else:
            if i == n_a:
                shared = _nsa_shared_kv(h, kv_norm_g, w_kv_shared, (cmp_pe_k, cmp_w1_k, cmp_w2_k),
                                        (cmp_pe_v, cmp_w1_v, cmp_w2_v), batch, seq)
            j = i - n_a
            h = _nsa_layer(h, attn_norm_g[i], b_w_qg[j], b_w_out[j], shared, batch, seq)
        h = _mlp(h, mlp_norm_g[i], mlp_w1[i].astype(BF16), mlp_w2[i].astype(BF16), name=f"mlp{i}")
        h = _ple(h, ple_norm_g[i], ple_w_gate[i].astype(BF16), p[i].reshape(m, -1),
                 ple_w_proj[i].astype(BF16), name=f"ple{i}")
    return _rmsnorm(h, final_norm_g, name="final_norm").reshape(batch, seq, d)
```

```python
import functools
import math

import jax
import jax.numpy as jnp
import numpy as np
from jax import lax
from jax.experimental import pallas as pl
from jax.experimental.pallas import tpu as pltpu

F32 = jnp.float32
BF16 = jnp.bfloat16

HEAD_DIM = 128
ROT_DIM = HEAD_DIM // 4
ROPE_THETA = 500000.0
NORM_EPS = 1e-6
NEG_INF = -1e30
FORCE_SCORE = 1e9
TINY = 1e-20
REMOVED_SCORE = -3.0e38
LOG2E = math.log2(math.e)

DILATED_GROUPS = ((128, 1), (512, 4), (2048, 16))
NSA_KV_GROUPS = 4
CMP_LEN = 32
CMP_STRIDE = 16
SLC_LEN = 64
SLC_TOP_N = 16
WIN_LEN = 512
N_NSA_BRANCH = 3

LANES = 128
SUBLANES = 8
MIB = 1024 * 1024


def _cparams(n_grid, vmem_mib):
    return pltpu.CompilerParams(dimension_semantics=("arbitrary",) * n_grid,
                                vmem_limit_bytes=vmem_mib * MIB)


def _nt_dot(a, b):
    return lax.dot_general(a, b, (((1,), (1,)), ((), ())), preferred_element_type=F32)


def _dot(a, b):
    return jnp.dot(a, b, preferred_element_type=F32)


def _perm_head_cols(w):
    k, n = w.shape
    half = ROT_DIM // 2
    w = w.reshape(k, n // HEAD_DIM, HEAD_DIM)
    w = jnp.concatenate([w[..., :half], w[..., ROT_DIM:LANES // 2 + half], w[..., half:ROT_DIM],
                         w[..., LANES // 2 + half:]], axis=-1)
    return w.reshape(k, n)


def _rope_table(pos, scale, identity=False):
    n = pos.shape[0]
    half = ROT_DIM // 2
    if identity:
        return jnp.stack([jnp.full((n, LANES), scale, F32), jnp.zeros((n, LANES), F32)])
    inv = ROPE_THETA ** (-jnp.arange(half, dtype=F32) * (2.0 / ROT_DIM))
    ang = pos.astype(F32)[:, None] * inv[None, :]
    c, s = jnp.cos(ang), jnp.sin(ang)
    ones = jnp.ones((n, LANES // 2 - half), F32)
    zeros = jnp.zeros((n, LANES // 2 - half), F32)
    cos = jnp.concatenate([c, ones, c, ones], axis=1)
    sg = jnp.concatenate([-s, zeros, s, zeros], axis=1)
    return jnp.stack([cos, sg]) * scale


def _apply_rope(a, cos, sg):
    return a * cos + pltpu.roll(a, LANES // 2, 1) * sg


def _rms_scale(x, g):
    ms = jnp.mean(x * x, axis=-1, keepdims=True)
    return (x * lax.rsqrt(ms + NORM_EPS) * g).astype(BF16)


def _norm_matmul_kernel(x_ref, g_ref, w_ref, *rest, rope, act):
    if rope:
        tab_ref, o_ref, xn_ref = rest
    else:
        o_ref, xn_ref = rest

    @pl.when(pl.program_id(1) == 0)
    def _():
        xn_ref[...] = _rms_scale(x_ref[...], g_ref[...])

    acc = _dot(xn_ref[...], w_ref[...])
    if rope:
        cos, sg = tab_ref[0, 0], tab_ref[0, 1]
        for c in range(acc.shape[1] // LANES):
            sl = slice(c * LANES, (c + 1) * LANES)
            o_ref[:, sl] = _apply_rope(acc[:, sl], cos, sg).astype(o_ref.dtype)
    else:
        if act == "sigmoid":
            acc = 1.0 / (1.0 + jnp.exp(-acc))
        o_ref[...] = acc.astype(o_ref.dtype)


def _norm_matmul(x, g, w, *, out_dtype, tables=None, tab_fn=None, seq=None, act=None,
                 tm=512, tn=512, name):
    m, k = x.shape
    n = w.shape[1]
    tm, tn = min(tm, m), min(tn, n)
    in_specs = [pl.BlockSpec((tm, k), lambda i, j: (i, 0)),
                pl.BlockSpec((1, k), lambda i, j: (0, 0)),
                pl.BlockSpec((k, tn), lambda i, j: (0, j))]
    args = [x, g.reshape(1, k), w]
    if tables is not None:
        nsb = seq // tm
        in_specs.append(pl.BlockSpec((1, 2, tm, LANES), lambda i, j: (tab_fn(j), 0, i % nsb, 0)))
        args.append(tables)
    return pl.pallas_call(
        functools.partial(_norm_matmul_kernel, rope=tables is not None, act=act),
        grid=(m // tm, n // tn),
        in_specs=in_specs,
        out_specs=pl.BlockSpec((tm, tn), lambda i, j: (i, j)),
        out_shape=jax.ShapeDtypeStruct((m, n), out_dtype),
        scratch_shapes=[pltpu.VMEM((tm, k), BF16)],
        compiler_params=_cparams(2, 40),
        name=name,
    )(*args)


def _matmul_resid_kernel(a_ref, w_ref, r_ref, o_ref):
    o_ref[...] = r_ref[...] + _dot(a_ref[...], w_ref[...])


def _matmul_resid(a, w, r, *, tm=512, tn=1024, name):
    m, k = a.shape
    n = w.shape[1]
    tm, tn = min(tm, m), min(tn, n)
    return pl.pallas_call(
        _matmul_resid_kernel,
        grid=(m // tm, n // tn),
        in_specs=[pl.BlockSpec((tm, k), lambda i, j: (i, 0)),
                  pl.BlockSpec((k, tn), lambda i, j: (0, j)),
                  pl.BlockSpec((tm, tn), lambda i, j: (i, j))],
        out_specs=pl.BlockSpec((tm, tn), lambda i, j: (i, j)),
        out_shape=jax.ShapeDtypeStruct((m, n), F32),
        compiler_params=_cparams(2, 40),
        name=name,
    )(a, w, r)


def _mlp_kernel(x_ref, g_ref, w1_ref, w2_ref, o_ref, xn_ref, acc_ref):
    f = pl.program_id(1)

    @pl.when(f == 0)
    def _():
        xn_ref[...] = _rms_scale(x_ref[...], g_ref[...])
        acc_ref[...] = jnp.zeros_like(acc_ref)

    a = _dot(xn_ref[...], w1_ref[...])
    a = jnp.square(jnp.maximum(a, 0.0)).astype(BF16)
    acc_ref[...] += _dot(a, w2_ref[...])

    @pl.when(f == pl.num_programs(1) - 1)
    def _():
        o_ref[...] = x_ref[...] + acc_ref[...]


def _mlp(h, g, w1, w2, *, tm=512, tf=1024, name):
    m, d = h.shape
    ff = w1.shape[1]
    tm, tf = min(tm, m), min(tf, ff)
    return pl.pallas_call(
        _mlp_kernel,
        grid=(m // tm, ff // tf),
        in_specs=[pl.BlockSpec((tm, d), lambda i, f: (i, 0)),
                  pl.BlockSpec((1, d), lambda i, f: (0, 0)),
                  pl.BlockSpec((d, tf), lambda i, f: (0, f)),
                  pl.BlockSpec((tf, d), lambda i, f: (f, 0))],
        out_specs=pl.BlockSpec((tm, d), lambda i, f: (i, 0)),
        out_shape=jax.ShapeDtypeStruct((m, d), F32),
        scratch_shapes=[pltpu.VMEM((tm, d), BF16), pltpu.VMEM((tm, d), F32)],
        compiler_params=_cparams(2, 56),
        name=name,
    )(h, g.reshape(1, d), w1, w2)


PLE_COL_CHUNK = 1024


def _ple_kernel(x_ref, g_ref, wg_ref, p_ref, wp_ref, *rest, final):
    if final:
        fg_ref, o_ref = rest
    else:
        (o_ref,) = rest
    n = o_ref.shape[1]
    xn = _rms_scale(x_ref[...], g_ref[...])
    pb = p_ref[...].astype(BF16)
    ssq = None
    for c0 in range(0, n, PLE_COL_CHUNK):
        sl = slice(c0, c0 + PLE_COL_CHUNK)
        gate = 1.0 / (1.0 + jnp.exp(-_dot(xn, wg_ref[:, sl])))
        hc = x_ref[:, sl] + _dot(pb, wp_ref[:, sl]) * gate
        o_ref[:, sl] = hc
        if final:
            part = jnp.sum(hc * hc, axis=-1, keepdims=True)
            ssq = part if ssq is None else ssq + part
    if final:
        o_ref[...] = o_ref[...] * lax.rsqrt(ssq * (1.0 / n) + NORM_EPS) * fg_ref[...]


def _ple(h, g, wg, p, wp, *, final_g=None, tm=512, name):
    m, d = h.shape
    dp = p.shape[1]
    n = wg.shape[1]
    assert n == d and n % PLE_COL_CHUNK == 0
    tm = min(tm, m)
    row = lambda i: (i, 0)
    const = lambda i: (0, 0)
    in_specs = [pl.BlockSpec((tm, d), row), pl.BlockSpec((1, d), const), pl.BlockSpec((d, n), const),
                pl.BlockSpec((tm, dp), row), pl.BlockSpec((dp, n), const)]
    args = [h, g.reshape(1, d), wg, p, wp]
    if final_g is not None:
        in_specs.append(pl.BlockSpec((1, n), const))
        args.append(final_g.reshape(1, n))
    return pl.pallas_call(
        functools.partial(_ple_kernel, final=final_g is not None),
        grid=(m // tm,),
        in_specs=in_specs,
        out_specs=pl.BlockSpec((tm, n), row),
        out_shape=jax.ShapeDtypeStruct((m, n), F32),
        compiler_params=_cparams(1, 56),
        name=name,
    )(*args)


def _rmsnorm_kernel(x_ref, g_ref, o_ref):
    x = x_ref[...]
    ms = jnp.mean(x * x, axis=-1, keepdims=True)
    o_ref[...] = x * lax.rsqrt(ms + NORM_EPS) * g_ref[...]


def _rmsnorm(h, g, *, tm=512, name):
    m, d = h.shape
    tm = min(tm, m)
    return pl.pallas_call(
        _rmsnorm_kernel,
        grid=(m // tm,),
        in_specs=[pl.BlockSpec((tm, d), lambda i: (i, 0)),
                  pl.BlockSpec((1, d), lambda i: (0, 0))],
        out_specs=pl.BlockSpec((tm, d), lambda i: (i, 0)),
        out_shape=jax.ShapeDtypeStruct((m, d), F32),
        compiler_params=_cparams(1, 32),
        name=name,
    )(h, g.reshape(1, d))


def _a_qkv_kernel(x_ref, g_ref, w_ref, tab_ref, o_ref, xn_ref, res_ref, *, dil):
    @pl.when(pl.program_id(1) == 0)
    def _():
        xn_ref[...] = _rms_scale(x_ref[...], g_ref[...])

    acc = _dot(xn_ref[...], w_ref[...])
    tm, tn = acc.shape
    cos, sg = tab_ref[0, 0], tab_ref[0, 1]
    for c in range(tn // LANES):
        sl = slice(c * LANES, (c + 1) * LANES)
        res = _apply_rope(acc[:, sl], cos, sg)
        if dil == 1:
            o_ref[0, 0, :, sl] = res.astype(BF16)
        else:
            res_ref[c] = res
            for r in range(dil):
                o_ref[0, r, :, sl] = res_ref[c, pl.ds(r, tm // dil, stride=dil), :].astype(BF16)


def _a_qkv(x, g, w, tables, *, batch, seq, dil, part_width, tm=512, tn=1024, name):
    m, k = x.shape
    n = w.shape[1]
    nsb = seq // tm
    bpp = part_width // tn
    assert tm % (dil * 16) == 0
    return pl.pallas_call(
        functools.partial(_a_qkv_kernel, dil=dil),
        grid=(m // tm, n // tn),
        in_specs=[pl.BlockSpec((tm, k), lambda i, j: (i, 0)),
                  pl.BlockSpec((1, k), lambda i, j: (0, 0)),
                  pl.BlockSpec((k, tn), lambda i, j: (0, j)),
                  pl.BlockSpec((1, 2, tm, LANES), lambda i, j: (j // bpp, 0, i % nsb, 0))],
        out_specs=pl.BlockSpec((1, dil, tm // dil, tn), lambda i, j: (i // nsb, 0, i % nsb, j)),
        out_shape=jax.ShapeDtypeStruct((batch, dil, seq // dil, n), BF16),
        scratch_shapes=[pltpu.VMEM((tm, k), BF16), pltpu.VMEM((tn // LANES, tm, LANES), F32)],
        compiler_params=_cparams(2, 40),
        name=name,
    )(x, g.reshape(1, k), w, tables)


def _dil_attn_kernel(q_ref, kp_ref, kc_ref, vp_ref, vc_ref, o_ref, lse_ref, *, reach, n_heads):
    i = pl.program_id(2)
    t = q_ref.shape[2]
    tp = kp_ref.shape[2]
    row = lax.broadcasted_iota(jnp.int32, (t, tp + t), 0)
    col = lax.broadcasted_iota(jnp.int32, (t, tp + t), 1)
    no_prev = jnp.where(i > 0, 0, reach + 1 + tp + t)
    dist = row + tp - col + jnp.where(col < tp, no_prev, 0)
    valid = dist.astype(jnp.uint32) <= reach
    lane = lax.broadcasted_iota(jnp.int32, (t, LANES), 1)
    lse_tile = jnp.zeros((t, LANES), F32)
    heads = [slice(h * LANES, (h + 1) * LANES) for h in range(n_heads)]
    scores = [_nt_dot(q_ref[0, 0, :, sl], jnp.concatenate([kp_ref[0, 0, :, sl], kc_ref[0, 0, :, sl]], axis=0))
              for sl in heads]
    for h, sl in enumerate(heads):
        v = jnp.concatenate([vp_ref[0, 0, :, sl], vc_ref[0, 0, :, sl]], axis=0)
        s = jnp.where(valid, scores[h], NEG_INF)
        m = jnp.max(s, axis=-1, keepdims=True)
        p = jnp.exp2(s - m)
        l = jnp.sum(p, axis=-1, keepdims=True)
        o_ref[0, 0, :, sl] = (_dot(p.astype(BF16), v) * (1.0 / l)).astype(o_ref.dtype)
        lse_tile = jnp.where(lane == h, m + jnp.log2(l), lse_tile)
    lse_ref[0, 0] = lse_tile


def _dil_attn(qkv, *, reach, n_heads, t, name):
    batch, dil, su, _ = qkv.shape
    w = n_heads * LANES
    tp = reach
    assert t % tp == 0 and tp % 16 == 0
    cur = lambda part: (lambda b, r, i: (b, r, i, part))
    prev = lambda part: (lambda b, r, i: (b, r, jnp.maximum(i * (t // tp) - 1, 0), part))
    return pl.pallas_call(
        functools.partial(_dil_attn_kernel, reach=reach, n_heads=n_heads),
        grid=(batch, dil, su // t),
        in_specs=[pl.BlockSpec((1, 1, t, w), cur(0)),
                  pl.BlockSpec((1, 1, tp, w), prev(1)),
                  pl.BlockSpec((1, 1, t, w), cur(1)),
                  pl.BlockSpec((1, 1, tp, w), prev(2)),
                  pl.BlockSpec((1, 1, t, w), cur(2))],
        out_specs=[pl.BlockSpec((1, 1, t, w), lambda b, r, i: (b, r, i, 0)),
                   pl.BlockSpec((1, 1, t, LANES), lambda b, r, i: (b, r, i, 0))],
        out_shape=[jax.ShapeDtypeStruct((batch, dil, su, w), BF16),
                   jax.ShapeDtypeStruct((batch, dil, su, LANES), F32)],
        compiler_params=_cparams(3, 40),
        name=name,
    )(qkv, qkv, qkv, qkv, qkv)


def _mix_out_kernel(*refs, dils, n_heads):
    n_g = len(dils)
    o_refs = refs[:n_g]
    lse_refs = refs[n_g:2 * n_g]
    w_ref, r_ref, out_ref, oc_ref, uo_ref, ul_ref = refs[2 * n_g:]
    tm = oc_ref.shape[0]

    @pl.when(pl.program_id(1) == 0)
    def _():
        for g, d in enumerate(dils):
            for r in range(d):
                rows = pl.ds(r, tm // d, stride=d)
                for h in range(n_heads):
                    uo_ref[g, h, rows, :] = o_refs[g][0, r, :, h * LANES:(h + 1) * LANES].astype(F32)
                ul_ref[g, rows, :] = lse_refs[g][0, r]
        m = ul_ref[0]
        for g in range(1, n_g):
            m = jnp.maximum(m, ul_ref[g])
        e = [jnp.exp2(ul_ref[g] - m) for g in range(n_g)]
        den = e[0]
        for g in range(1, n_g):
            den = den + e[g]
        inv = 1.0 / den
        for h in range(n_heads):
            sl = slice(h * LANES, (h + 1) * LANES)
            acc = (e[0] * inv)[:, h:h + 1] * uo_ref[0, h]
            for g in range(1, n_g):
                acc = acc + (e[g] * inv)[:, h:h + 1] * uo_ref[g, h]
            oc_ref[:, sl] = acc.astype(BF16)

    out_ref[...] = r_ref[...] + _dot(oc_ref[...], w_ref[...])


def _mix_out(os_, lses, w, r, *, seq, n_heads, tm=512, tn=1024, name):
    m, n = r.shape
    k = w.shape[0]
    dils = tuple(o.shape[1] for o in os_)
    nsb = seq // tm
    tn = min(tn, n)
    cls = lambda i, j: (i // nsb, 0, i % nsb, 0)
    return pl.pallas_call(
        functools.partial(_mix_out_kernel, dils=dils, n_heads=n_heads),
        grid=(m // tm, n // tn),
        in_specs=([pl.BlockSpec((1, d, tm // d, k), cls) for d in dils]
                  + [pl.BlockSpec((1, d, tm // d, LANES), cls) for d in dils]
                  + [pl.BlockSpec((k, tn), lambda i, j: (0, j)),
                     pl.BlockSpec((tm, tn), lambda i, j: (i, j))]),
        out_specs=pl.BlockSpec((tm, tn), lambda i, j: (i, j)),
        out_shape=jax.ShapeDtypeStruct((m, n), F32),
        scratch_shapes=[pltpu.VMEM((tm, k), BF16), pltpu.VMEM((len(dils), n_heads, tm, LANES), F32),
                        pltpu.VMEM((len(dils), tm, LANES), F32)],
        compiler_params=_cparams(2, 48),
        name=name,
    )(*os_, *lses, w, r)


def _cmp_kernel(z_ref, w1_ref, w2_ref, pe_ref, tab_ref, o_ref, *, n_cmp):
    w1 = w1_ref[0]
    bias = _dot(pe_ref[0].astype(BF16), w1)[0:1]
    hid = _dot(z_ref[0, 0], w1) + bias
    hid = 0.5 * hid * (1.0 + jnp.tanh(math.sqrt(2.0 / math.pi) * (hid + 0.044715 * (hid * hid * hid))))
    o = _apply_rope(_dot(hid.astype(BF16), w2_ref[0]), tab_ref[0, 0], tab_ref[0, 1])
    row = lax.broadcasted_iota(jnp.int32, o.shape, 0)
    o_ref[0, 0] = jnp.where(row < n_cmp, o, 0.0).astype(o_ref.dtype)


def _compress(z, w1, w2, pe, tab, *, n_cmp, name):
    two, bg, nc, ld = z.shape
    hid = w1.shape[2]
    return pl.pallas_call(
        functools.partial(_cmp_kernel, n_cmp=n_cmp),
        grid=(two, bg),
        in_specs=[pl.BlockSpec((1, 1, nc, ld), lambda s, b: (s, b, 0, 0)),
                  pl.BlockSpec((1, ld, hid), lambda s, b: (s, 0, 0)),
                  pl.BlockSpec((1, hid, LANES), lambda s, b: (s, 0, 0)),
                  pl.BlockSpec((1, SUBLANES, ld), lambda s, b: (s, 0, 0)),
                  pl.BlockSpec((1, 2, nc, LANES), lambda s, b: (s, 0, 0, 0))],
        out_specs=pl.BlockSpec((1, 1, nc, LANES), lambda s, b: (s, b, 0, 0)),
        out_shape=jax.ShapeDtypeStruct((two, bg, nc, LANES), BF16),
        compiler_params=_cparams(2, 40),
        name=name,
    )(z, w1, w2, pe, tab)


def _nsa_kernel(q_ref, gate_ref, kc_ref, vct_ref, ks_ref, vst_ref, kw_ref, vwt_ref, ovt_ref, o_ref, kaug_ref,
                acc_ref, *, tq, n_sub, tk, hpg, top_n):
    qi = pl.program_id(2)
    r = hpg * tq
    seq = ks_ref.shape[1]
    nj = ovt_ref.shape[0]
    nc = kc_ref.shape[2]

    @pl.when(qi == 0)
    def _():
        def fill(c, carry):
            r0 = pl.multiple_of(c * tk, tk)
            kaug_ref[pl.ds(r0, tk), 0:LANES] = ks_ref[0, pl.ds(r0, tk), :]
            blk = (r0 + lax.broadcasted_iota(jnp.int32, (tk, nj), 0)) // SLC_LEN
            hot = jnp.where(blk == lax.broadcasted_iota(jnp.int32, (tk, nj), 1), 1.0, 0.0)
            kaug_ref[pl.ds(r0, tk), LANES:LANES + nj] = hot.astype(BF16)
            return carry
        lax.fori_loop(0, seq // tk, fill, 0)

    subs = range(n_sub)
    n_wt = WIN_LEN // tq + 1
    span = n_wt * tq
    t0 = [(qi * n_sub + u) * tq for u in subs]
    qs, tcol = [], []
    for u in subs:
        q = q_ref[0, u * tq:(u + 1) * tq, :]
        qs.append(jnp.concatenate([q[:, h * LANES:(h + 1) * LANES] for h in range(hpg)], axis=0))
        tcol.append(t0[u] + (lax.broadcasted_iota(jnp.int32, (1, r), 1) & (tq - 1)))

    s_c = [_nt_dot(kc_ref[0, 0], qs[u]) for u in subs]
    start = [pl.multiple_of(jnp.maximum(t0[u] - WIN_LEN, 0), tq) for u in subs]
    s_w = [_nt_dot(kw_ref[0, pl.ds(start[u], span), :], qs[u]) for u in subs]

    cend = lax.broadcasted_iota(jnp.int32, (nc, 1), 0) * CMP_STRIDE + (CMP_LEN - 1)
    ovt = ovt_ref[...]
    o_cmp, imp = [], []
    for u in subs:
        sc = jnp.where(cend <= tcol[u], s_c[u], NEG_INF)
        m_c = jnp.max(sc, axis=0, keepdims=True)
        e_c = jnp.exp2(sc - m_c)
        inv_c = jnp.where(m_c > 0.5 * NEG_INF, 1.0 / jnp.maximum(jnp.sum(e_c, axis=0, keepdims=True), TINY), 0.0)
        p_c = e_c * inv_c
        o_cmp.append(_dot(vct_ref[0, 0], p_c.astype(BF16)))
        psum = p_c[:, 0:tq]
        for h in range(1, hpg):
            psum = psum + p_c[:, h * tq:(h + 1) * tq]
        p_hi = psum.astype(BF16)
        rem = psum - p_hi.astype(F32)
        p_mid = rem.astype(BF16)
        p_lo = (rem - p_mid.astype(F32)).astype(BF16)
        imp.append(_dot(ovt, p_hi) + _dot(ovt, p_mid) + _dot(ovt, p_lo))

    acc_w = []
    for u in subs:
        dist = tcol[u] - (start[u] + lax.broadcasted_iota(jnp.int32, (span, 1), 0))
        sw = jnp.where(dist.astype(jnp.uint32) < WIN_LEN, s_w[u], NEG_INF)
        p_w = jnp.exp2(sw - jnp.max(sw, axis=0, keepdims=True)).astype(BF16)
        wt0 = start[u] // tq
        vw = jnp.concatenate([vwt_ref[0, 0, wt0 + i] for i in range(n_wt)], axis=1)
        acc_w.append(_dot(vw, p_w))

    jrow = lax.broadcasted_iota(jnp.int32, (nj, tq), 0)
    score = []
    for u in subs:
        tok = t0[u] + lax.broadcasted_iota(jnp.int32, (nj, tq), 1)
        cur = tok // SLC_LEN
        forced = (jrow == 0) | (jrow == cur) | (jrow == cur - 1)
        score.append(jnp.where(forced, FORCE_SCORE, jnp.where(jrow * SLC_LEN <= tok, imp[u], NEG_INF)))
    for _ in range(top_n):
        for u in subs:
            mx = jnp.max(score[u], axis=0, keepdims=True)
            first = jnp.min(jnp.where(score[u] == mx, jrow, nj), axis=0, keepdims=True)
            score[u] = jnp.where(jrow == first, REMOVED_SCORE, score[u])
    q_aug = []
    for u in subs:
        bias = jnp.where(score[u] == REMOVED_SCORE, 0.0, NEG_INF).T.astype(BF16)
        q_aug.append(jnp.concatenate([qs[u], jnp.concatenate([bias] * hpg, axis=0)], axis=1))

    def slc_tile(kj, m_run, diagonal):
        k0 = pl.multiple_of(kj * tk, tk)
        ka = kaug_ref[pl.ds(k0, tk), :]
        vt = vst_ref[0, 0, kj]
        out = []
        scores = [_nt_dot(ka, q_aug[u]) for u in subs]
        for u in subs:
            s = scores[u]
            if diagonal:
                s = jnp.where(k0 + lax.broadcasted_iota(jnp.int32, (tk, 1), 0) <= tcol[u], s, NEG_INF)
            m_new = jnp.maximum(m_run[u], jnp.max(s, axis=0, keepdims=True))
            p = jnp.exp2(s - m_new).astype(BF16)
            acc_ref[u] = jnp.exp2(m_run[u] - m_new) * acc_ref[u] + _dot(vt, p)
            out.append(m_new)
        return tuple(out)

    n_full = (qi * n_sub * tq) // tk
    for u in subs:
        acc_ref[u] = jnp.zeros((LANES + SUBLANES, r), F32)
    m_run = (jnp.full((1, r), NEG_INF, F32),) * n_sub
    m_run = lax.fori_loop(0, n_full, lambda kj, c: slc_tile(kj, c, False), m_run)
    slc_tile(n_full, m_run, True)

    for u in subs:
        acc_s = acc_ref[u]
        g_t = gate_ref[0, u * tq:(u + 1) * tq, :].T

        def gate_row(br):
            return jnp.concatenate([g_t[br * hpg + h:br * hpg + h + 1, :] for h in range(hpg)], axis=1)

        o_t = (gate_row(0) * o_cmp[u]
               + (gate_row(1) * (1.0 / acc_s[LANES:LANES + 1])) * acc_s[0:LANES]
               + (gate_row(2) * (1.0 / acc_w[u][LANES:LANES + 1])) * acc_w[u][0:LANES])
        for h in range(hpg):
            o_ref[0, u * tq:(u + 1) * tq, h * LANES:(h + 1) * LANES] = (
                o_t[:, h * tq:(h + 1) * tq].T.astype(o_ref.dtype))


def _nsa(q, gates, kv, kc, vct, vst, vwt, overlap_t, *, n_groups, hpg, top_n, tq, n_sub, tk, name):
    b, s, _ = q.shape
    nc = kc.shape[2]
    nj = overlap_t.shape[0]
    g_ = n_groups
    seg = lambda sidx: (lambda bi, gi, qi: (bi, 0, sidx * g_ + gi))
    bg = lambda bi, gi, qi: (bi * g_ + gi, 0, 0, 0)
    vrows = LANES + SUBLANES
    tqs = n_sub * tq
    assert tk % tqs == 0 and s % tk == 0
    return pl.pallas_call(
        functools.partial(_nsa_kernel, tq=tq, n_sub=n_sub, tk=tk, hpg=hpg, top_n=top_n),
        grid=(b, g_, s // tqs),
        in_specs=[pl.BlockSpec((1, tqs, hpg * LANES), lambda bi, gi, qi: (bi, qi, gi)),
                  pl.BlockSpec((1, tqs, LANES), lambda bi, gi, qi: (bi, qi, gi)),
                  pl.BlockSpec((1, 1, nc, LANES), bg),
                  pl.BlockSpec((1, 1, LANES, nc), bg),
                  pl.BlockSpec((1, s, LANES), seg(KV_K_SLC)),
                  pl.BlockSpec((1, 1, s // tk, vrows, tk), lambda bi, gi, qi: (bi, gi, 0, 0, 0)),
                  pl.BlockSpec((1, s, LANES), seg(KV_K_WIN)),
                  pl.BlockSpec((1, 1, s // tq, vrows, tq), lambda bi, gi, qi: (bi, gi, 0, 0, 0)),
                  pl.BlockSpec((nj, nc), lambda bi, gi, qi: (0, 0))],
        out_specs=pl.BlockSpec((1, tqs, hpg * LANES), lambda bi, gi, qi: (bi, qi, gi)),
        out_shape=jax.ShapeDtypeStruct(q.shape, BF16),
        scratch_shapes=[pltpu.VMEM((s, LANES + nj), BF16),
                        pltpu.VMEM((n_sub, vrows, hpg * tq), F32)],
        compiler_params=_cparams(3, 48),
        name=name,
    )(q, gates, kc, vct, kv, vst, kv, vwt, overlap_t)


KV_SEGS = (2, 4, 0, 1, 3, 5)
KV_ROPE_SRC = (2, 4)
KV_K_SLC, KV_K_WIN, KV_K_CMP, KV_V_CMP, KV_V_SLC, KV_V_WIN = range(6)

NSA_TQ = 128
NSA_SUB = 4
NSA_TK = 512


def _dilated_layer(h, g_attn, w_in, w_out, batch, seq):
    m, d = h.shape
    ng = len(DILATED_GROUPS)
    dils = tuple(dl for _, dl in DILATED_GROUPS)
    hg = w_out.shape[0] // HEAD_DIM
    wd = hg * HEAD_DIM
    reach = DILATED_GROUPS[0][0] // DILATED_GROUPS[0][1]
    assert all(win // dl == reach for win, dl in DILATED_GROUPS)
    w4 = w_in.reshape(d, 3, ng, wd)
    ws = [jnp.concatenate([_perm_head_cols(w4[:, 0, gi]), _perm_head_cols(w4[:, 1, gi]), w4[:, 2, gi]],
                          axis=1).astype(BF16) for gi in range(ng)]
    pos = jnp.arange(seq)
    tables = jnp.stack([_rope_table(pos, HEAD_DIM ** -0.5 * LOG2E), _rope_table(pos, 1.0),
                        _rope_table(pos, 1.0, identity=True)])
    tm = min(512, seq)
    os_, lses = [], []
    for gi, dil in enumerate(dils):
        qkv = _a_qkv(h, g_attn, ws[gi], tables, batch=batch, seq=seq, dil=dil,
                     part_width=wd, tm=tm, name=f"a_qkv{gi}")
        o, lse = _dil_attn(qkv, reach=reach, n_heads=hg, t=min(256, seq // dil), name=f"a_attn{gi}")
        os_.append(o)
        lses.append(lse)
    return _mix_out(os_, lses, w_out.astype(BF16), h, seq=seq, n_heads=hg, tm=tm, name="a_out")


def _value_tiles_t(v, batch, seq, g_, tile):
    vt = v.reshape(batch, seq // tile, tile, g_, HEAD_DIM).transpose(0, 3, 1, 4, 2)
    ones = jnp.ones((batch, g_, seq // tile, SUBLANES, tile), BF16)
    return jnp.concatenate([vt, ones], axis=3)


def _nsa_shared_kv(h, kv_norm_g, w_kv, cmp_k, cmp_v, batch, seq):
    g_ = NSA_KV_GROUPS
    pos = jnp.arange(seq)
    seg_w = g_ * HEAD_DIM
    (pe_k, w1_k, w2_k), (pe_v, w1_v, w2_v) = cmp_k, cmp_v
    src = lambda s: w_kv[:, s * seg_w:(s + 1) * seg_w]
    w = jnp.concatenate([_perm_head_cols(src(s)) if s in KV_ROPE_SRC else src(s) for s in KV_SEGS],
                        axis=1).astype(BF16)
    kv_tables = jnp.stack([_rope_table(pos, 1.0), _rope_table(pos, 1.0, identity=True)])
    kv = _norm_matmul(h, kv_norm_g, w, out_dtype=BF16, tables=kv_tables, seq=seq,
                      tab_fn=lambda j: jnp.minimum(j, 1), tn=len(KV_ROPE_SRC) * seg_w, name="b_kv")
    n_chunk = seq // CMP_STRIDE
    n_cmp = (seq - CMP_LEN) // CMP_STRIDE + 1
    assert CMP_LEN == 2 * CMP_STRIDE

    def blocks(seg_idx):
        c = kv[:, seg_idx * seg_w:(seg_idx + 1) * seg_w].reshape(batch, n_chunk, CMP_STRIDE, g_, HEAD_DIM)
        c = c.transpose(0, 3, 1, 2, 4).reshape(batch * g_, n_chunk, CMP_STRIDE * HEAD_DIM)
        return jnp.concatenate([c, jnp.roll(c, -1, axis=1)], axis=-1)

    z = jnp.stack([blocks(KV_K_CMP), blocks(KV_V_CMP)])
    pe = jnp.stack([pe_k.reshape(1, -1), pe_v.reshape(1, -1)])
    pe = jnp.broadcast_to(pe, (2, SUBLANES, pe.shape[-1]))
    cmp_tab = jnp.stack([_rope_table(jnp.arange(n_chunk) * CMP_STRIDE + CMP_LEN - 1, 1.0),
                         _rope_table(jnp.arange(n_chunk), 1.0, identity=True)])
    cmp_kv = _compress(z, jnp.stack([w1_k, w1_v]).astype(BF16),
                       jnp.stack([_perm_head_cols(w2_k), w2_v]).astype(BF16),
                       pe, cmp_tab, n_cmp=n_cmp, name="b_compress")
    kc = cmp_kv[0][:, None]
    vct = cmp_kv[1].transpose(0, 2, 1)[:, None]
    vst = _value_tiles_t(kv[:, KV_V_SLC * seg_w:(KV_V_SLC + 1) * seg_w], batch, seq, g_, min(NSA_TK, seq))
    vwt = _value_tiles_t(kv[:, KV_V_WIN * seg_w:(KV_V_WIN + 1) * seg_w], batch, seq, g_, NSA_TQ)
    return kv, kc, vct, vst, vwt


def _nsa_layer(h, g_attn, w_qg, w_out, shared, batch, seq):
    m, d = h.shape
    g_ = NSA_KV_GROUPS
    n_heads = w_out.shape[0] // HEAD_DIM
    hpg = n_heads // g_
    seg_w = g_ * HEAD_DIM
    pos = jnp.arange(seq)
    kv, kc, vct, vst, vwt = shared
    n_chunk = seq // CMP_STRIDE
    n_cmp = (seq - CMP_LEN) // CMP_STRIDE + 1
    n_q = n_heads * HEAD_DIM
    q_tables = jnp.stack([_rope_table(pos, HEAD_DIM ** -0.5 * LOG2E)])
    q = _norm_matmul(h, g_attn, _perm_head_cols(w_qg[:, :n_q]).astype(BF16), out_dtype=BF16,
                     tables=q_tables, seq=seq, tab_fn=lambda j: 0, tn=1024, name="b_q")
    wg = w_qg[:, n_q:].reshape(d, N_NSA_BRANCH, g_, hpg).transpose(0, 2, 1, 3).reshape(d, g_, N_NSA_BRANCH * hpg)
    wg = jnp.pad(wg, ((0, 0), (0, 0), (0, LANES - N_NSA_BRANCH * hpg))).reshape(d, g_ * LANES)
    gates = _norm_matmul(h, g_attn, wg.astype(BF16), out_dtype=F32, act="sigmoid", tn=g_ * LANES,
                         name="b_gates")

    n_slc = seq // SLC_LEN
    nj = -(-n_slc // LANES) * LANES
    cs = np.arange(n_chunk)[:, None] * CMP_STRIDE
    ss = np.arange(nj)[None, :] * SLC_LEN
    ov = np.clip(np.minimum(cs + CMP_LEN, ss + SLC_LEN) - np.maximum(cs, ss), 0, None) / CMP_LEN
    ov[n_cmp:, :] = 0.0
    overlap_t = jnp.asarray(ov.T, BF16)

    o = _nsa(q.reshape(batch, seq, n_q), gates.reshape(batch, seq, g_ * LANES),
             kv.reshape(batch, seq, 6 * seg_w), kc, vct, vst, vwt, overlap_t,
             n_groups=g_, hpg=hpg, top_n=min(SLC_TOP_N, n_slc), tq=NSA_TQ, n_sub=NSA_SUB, tk=min(NSA_TK, seq),
             name="b_nsa")
    return _matmul_resid(o.reshape(m, n_q), w_out.astype(BF16), h, name="b_out")


def kernel(x, p, a_w_in, a_w_out, b_w_qg, b_w_out, kv_norm_g, w_kv_shared, cmp_pe_k, cmp_w1_k, cmp_w2_k,
           cmp_pe_v, cmp_w1_v, cmp_w2_v, attn_norm_g, mlp_norm_g, mlp_w1, mlp_w2, ple_norm_g, ple_w_gate,
           ple_w_proj, final_norm_g):
    batch, seq, d = x.shape
    m = batch * seq
    depth = attn_norm_g.shape[0]
    n_a = a_w_in.shape[0]
    h = x.reshape(m, d)
    shared = None
    for i in range(depth):
        if i < n_a:
            h = _dilated_layer(h, attn_norm_g[i], a_w_in[i], a_w_out[i], batch, seq)
        else:
            if i == n_a:
                shared = _nsa_shared_kv(h, kv_norm_g, w_kv_shared, (cmp_pe_k, cmp_w1_k, cmp_w2_k),
                                        (cmp_pe_v, cmp_w1_v, cmp_w2_v), batch, seq)
            j = i - n_a
            h = _nsa_layer(h, attn_norm_g[i], b_w_qg[j], b_w_out[j], shared, batch, seq)
        h = _mlp(h, mlp_norm_g[i], mlp_w1[i].astype(BF16), mlp_w2[i].astype(BF16), name=f"mlp{i}")
        h = _ple(h, ple_norm_g[i], ple_w_gate[i].astype(BF16), p[i].reshape(m, -1),
                 ple_w_proj[i].astype(BF16), final_g=final_norm_g if i == depth - 1 else None, name=f"ple{i}")
    if depth == 0:
        h = _rmsnorm(h, final_norm_g, name="final_norm")
    return h.reshape(batch, seq, d)
```

```python
import functools
import math

import jax
import jax.numpy as jnp
import numpy as np
from jax import lax
from jax.experimental import pallas as pl
from jax.experimental.pallas import tpu as pltpu

F32 = jnp.float32
BF16 = jnp.bfloat16

HEAD_DIM = 128
ROT_DIM = HEAD_DIM // 4
ROPE_THETA = 500000.0
NORM_EPS = 1e-6
NEG_INF = -1e30
FORCE_SCORE = 1e9
TINY = 1e-20
REMOVED_SCORE = -3.0e38
LOG2E = math.log2(math.e)

DILATED_GROUPS = ((128, 1), (512, 4), (2048, 16))
NSA_KV_GROUPS = 4
CMP_LEN = 32
CMP_STRIDE = 16
SLC_LEN = 64
SLC_TOP_N = 16
WIN_LEN = 512
N_NSA_BRANCH = 3

LANES = 128
SUBLANES = 8
MIB = 1024 * 1024


def _cparams(n_grid, vmem_mib):
    return pltpu.CompilerParams(dimension_semantics=("arbitrary",) * n_grid,
                                vmem_limit_bytes=vmem_mib * MIB)


def _nt_dot(a, b):
    return lax.dot_general(a, b, (((1,), (1,)), ((), ())), preferred_element_type=F32)


def _dot(a, b):
    return jnp.dot(a, b, preferred_element_type=F32)


def _perm_head_cols(w):
    k, n = w.shape
    half = ROT_DIM // 2
    w = w.reshape(k, n // HEAD_DIM, HEAD_DIM)
    w = jnp.concatenate([w[..., :half], w[..., ROT_DIM:LANES // 2 + half], w[..., half:ROT_DIM],
                         w[..., LANES // 2 + half:]], axis=-1)
    return w.reshape(k, n)


def _rope_tables(pos, scales):
    n = pos.shape[0]
    half = ROT_DIM // 2
    inv = ROPE_THETA ** (-jnp.arange(half, dtype=F32) * (2.0 / ROT_DIM))
    ang = pos.astype(F32)[:, None] * inv[None, :]
    c, s = jnp.cos(ang), jnp.sin(ang)
    ones = jnp.ones((n, LANES // 2 - half), F32)
    zeros = jnp.zeros((n, LANES // 2 - half), F32)
    rot = jnp.stack([jnp.concatenate([c, ones, c, ones], axis=1),
                     jnp.concatenate([-s, zeros, s, zeros], axis=1)])
    ident = jnp.stack([jnp.ones((n, LANES), F32), jnp.zeros((n, LANES), F32)])
    return jnp.stack([rot * sc for sc in scales] + [ident])


ROPE_Q, ROPE_K, ROPE_NONE = range(3)


def _seq_rope_tables(seq):
    return _rope_tables(jnp.arange(seq), (HEAD_DIM ** -0.5 * LOG2E, 1.0))


def _apply_rope(a, cos, sg):
    return a * cos + pltpu.roll(a, LANES // 2, 1) * sg


def _rms_scale(x, g):
    ms = jnp.mean(x * x, axis=-1, keepdims=True)
    return (x * lax.rsqrt(ms + NORM_EPS) * g).astype(BF16)


def _norm_matmul_kernel(x_ref, g_ref, w_ref, *rest, rope, act):
    if rope:
        tab_ref, o_ref, xn_ref = rest
    else:
        o_ref, xn_ref = rest

    @pl.when(pl.program_id(1) == 0)
    def _():
        xn_ref[...] = _rms_scale(x_ref[...], g_ref[...])

    acc = _dot(xn_ref[...], w_ref[...])
    if rope:
        cos, sg = tab_ref[0, 0], tab_ref[0, 1]
        for c in range(acc.shape[1] // LANES):
            sl = slice(c * LANES, (c + 1) * LANES)
            o_ref[:, sl] = _apply_rope(acc[:, sl], cos, sg).astype(o_ref.dtype)
    else:
        if act == "sigmoid":
            acc = 1.0 / (1.0 + jnp.exp(-acc))
        o_ref[...] = acc.astype(o_ref.dtype)


def _norm_matmul(x, g, w, *, out_dtype, tables=None, tab_fn=None, seq=None, act=None,
                 tm=512, tn=512, name):
    m, k = x.shape
    n = w.shape[1]
    tm, tn = min(tm, m), min(tn, n)
    in_specs = [pl.BlockSpec((tm, k), lambda i, j: (i, 0)),
                pl.BlockSpec((1, k), lambda i, j: (0, 0)),
                pl.BlockSpec((k, tn), lambda i, j: (0, j))]
    args = [x, g.reshape(1, k), w]
    if tables is not None:
        nsb = seq // tm
        in_specs.append(pl.BlockSpec((1, 2, tm, LANES), lambda i, j: (tab_fn(j), 0, i % nsb, 0)))
        args.append(tables)
    return pl.pallas_call(
        functools.partial(_norm_matmul_kernel, rope=tables is not None, act=act),
        grid=(m // tm, n // tn),
        in_specs=in_specs,
        out_specs=pl.BlockSpec((tm, tn), lambda i, j: (i, j)),
        out_shape=jax.ShapeDtypeStruct((m, n), out_dtype),
        scratch_shapes=[pltpu.VMEM((tm, k), BF16)],
        compiler_params=_cparams(2, 40),
        name=name,
    )(*args)


def _matmul_resid_kernel(a_ref, w_ref, r_ref, o_ref):
    o_ref[...] = r_ref[...] + _dot(a_ref[...], w_ref[...])


def _matmul_resid(a, w, r, *, tm=512, tn=1024, name):
    m, k = a.shape
    n = w.shape[1]
    tm, tn = min(tm, m), min(tn, n)
    return pl.pallas_call(
        _matmul_resid_kernel,
        grid=(m // tm, n // tn),
        in_specs=[pl.BlockSpec((tm, k), lambda i, j: (i, 0)),
                  pl.BlockSpec((k, tn), lambda i, j: (0, j)),
                  pl.BlockSpec((tm, tn), lambda i, j: (i, j))],
        out_specs=pl.BlockSpec((tm, tn), lambda i, j: (i, j)),
        out_shape=jax.ShapeDtypeStruct((m, n), F32),
        compiler_params=_cparams(2, 40),
        name=name,
    )(a, w, r)


def _mlp_kernel(x_ref, g_ref, w1_ref, w2_ref, o_ref, xn_ref, acc_ref):
    f = pl.program_id(1)

    @pl.when(f == 0)
    def _():
        xn_ref[...] = _rms_scale(x_ref[...], g_ref[...])
        acc_ref[...] = jnp.zeros_like(acc_ref)

    a = _dot(xn_ref[...], w1_ref[...])
    a = jnp.square(jnp.maximum(a, 0.0)).astype(BF16)
    acc_ref[...] += _dot(a, w2_ref[...])

    @pl.when(f == pl.num_programs(1) - 1)
    def _():
        o_ref[...] = x_ref[...] + acc_ref[...]


def _mlp(h, g, w1, w2, *, tm=512, tf=1024, name):
    m, d = h.shape
    ff = w1.shape[1]
    tm, tf = min(tm, m), min(tf, ff)
    return pl.pallas_call(
        _mlp_kernel,
        grid=(m // tm, ff // tf),
        in_specs=[pl.BlockSpec((tm, d), lambda i, f: (i, 0)),
                  pl.BlockSpec((1, d), lambda i, f: (0, 0)),
                  pl.BlockSpec((d, tf), lambda i, f: (0, f)),
                  pl.BlockSpec((tf, d), lambda i, f: (f, 0))],
        out_specs=pl.BlockSpec((tm, d), lambda i, f: (i, 0)),
        out_shape=jax.ShapeDtypeStruct((m, d), F32),
        scratch_shapes=[pltpu.VMEM((tm, d), BF16), pltpu.VMEM((tm, d), F32)],
        compiler_params=_cparams(2, 56),
        name=name,
    )(h, g.reshape(1, d), w1, w2)


PLE_COL_CHUNK = 1024


def _ple_kernel(x_ref, g_ref, wg_ref, p_ref, wp_ref, *rest, final):
    if final:
        fg_ref, o_ref = rest
    else:
        (o_ref,) = rest
    n = o_ref.shape[1]
    xn = _rms_scale(x_ref[...], g_ref[...])
    pb = p_ref[...].astype(BF16)
    ssq = None
    for c0 in range(0, n, PLE_COL_CHUNK):
        sl = slice(c0, c0 + PLE_COL_CHUNK)
        gate = 1.0 / (1.0 + jnp.exp(-_dot(xn, wg_ref[:, sl])))
        hc = x_ref[:, sl] + _dot(pb, wp_ref[:, sl]) * gate
        o_ref[:, sl] = hc
        if final:
            part = jnp.sum(hc * hc, axis=-1, keepdims=True)
            ssq = part if ssq is None else ssq + part
    if final:
        o_ref[...] = o_ref[...] * lax.rsqrt(ssq * (1.0 / n) + NORM_EPS) * fg_ref[...]


def _ple(h, g, wg, p, wp, *, final_g=None, tm=512, name):
    m, d = h.shape
    dp = p.shape[1]
    n = wg.shape[1]
    assert n == d and n % PLE_COL_CHUNK == 0
    tm = min(tm, m)
    row = lambda i: (i, 0)
    const = lambda i: (0, 0)
    in_specs = [pl.BlockSpec((tm, d), row), pl.BlockSpec((1, d), const), pl.BlockSpec((d, n), const),
                pl.BlockSpec((tm, dp), row), pl.BlockSpec((dp, n), const)]
    args = [h, g.reshape(1, d), wg, p, wp]
    if final_g is not None:
        in_specs.append(pl.BlockSpec((1, n), const))
        args.append(final_g.reshape(1, n))
    return pl.pallas_call(
        functools.partial(_ple_kernel, final=final_g is not None),
        grid=(m // tm,),
        in_specs=in_specs,
        out_specs=pl.BlockSpec((tm, n), row),
        out_shape=jax.ShapeDtypeStruct((m, n), F32),
        compiler_params=_cparams(1, 56),
        name=name,
    )(*args)


def _rmsnorm_kernel(x_ref, g_ref, o_ref):
    x = x_ref[...]
    ms = jnp.mean(x * x, axis=-1, keepdims=True)
    o_ref[...] = x * lax.rsqrt(ms + NORM_EPS) * g_ref[...]


def _rmsnorm(h, g, *, tm=512, name):
    m, d = h.shape
    tm = min(tm, m)
    return pl.pallas_call(
        _rmsnorm_kernel,
        grid=(m // tm,),
        in_specs=[pl.BlockSpec((tm, d), lambda i: (i, 0)),
                  pl.BlockSpec((1, d), lambda i: (0, 0))],
        out_specs=pl.BlockSpec((tm, d), lambda i: (i, 0)),
        out_shape=jax.ShapeDtypeStruct((m, d), F32),
        compiler_params=_cparams(1, 32),
        name=name,
    )(h, g.reshape(1, d))


def _a_qkv_kernel(x_ref, g_ref, w_ref, tab_ref, o_ref, xn_ref, res_ref, *, dil):
    @pl.when(pl.program_id(1) == 0)
    def _():
        xn_ref[...] = _rms_scale(x_ref[...], g_ref[...])

    acc = _dot(xn_ref[...], w_ref[...])
    tm, tn = acc.shape
    cos, sg = tab_ref[0, 0], tab_ref[0, 1]
    for c in range(tn // LANES):
        sl = slice(c * LANES, (c + 1) * LANES)
        res = _apply_rope(acc[:, sl], cos, sg)
        if dil == 1:
            o_ref[0, 0, :, sl] = res.astype(BF16)
        else:
            res_ref[c] = res
            for r in range(dil):
                o_ref[0, r, :, sl] = res_ref[c, pl.ds(r, tm // dil, stride=dil), :].astype(BF16)


def _a_qkv(x, g, w, tables, *, batch, seq, dil, part_width, tm=512, tn=1024, name):
    m, k = x.shape
    n = w.shape[1]
    nsb = seq // tm
    bpp = part_width // tn
    assert tm % (dil * 16) == 0
    return pl.pallas_call(
        functools.partial(_a_qkv_kernel, dil=dil),
        grid=(m // tm, n // tn),
        in_specs=[pl.BlockSpec((tm, k), lambda i, j: (i, 0)),
                  pl.BlockSpec((1, k), lambda i, j: (0, 0)),
                  pl.BlockSpec((k, tn), lambda i, j: (0, j)),
                  pl.BlockSpec((1, 2, tm, LANES), lambda i, j: (j // bpp, 0, i % nsb, 0))],
        out_specs=pl.BlockSpec((1, dil, tm // dil, tn), lambda i, j: (i // nsb, 0, i % nsb, j)),
        out_shape=jax.ShapeDtypeStruct((batch, dil, seq // dil, n), BF16),
        scratch_shapes=[pltpu.VMEM((tm, k), BF16), pltpu.VMEM((tn // LANES, tm, LANES), F32)],
        compiler_params=_cparams(2, 40),
        name=name,
    )(x, g.reshape(1, k), w, tables)


def _dil_attn_kernel(q_ref, kp_ref, kc_ref, vp_ref, vc_ref, o_ref, lse_ref, *, reach, n_heads):
    i = pl.program_id(2)
    t = q_ref.shape[2]
    tp = kp_ref.shape[2]
    row = lax.broadcasted_iota(jnp.int32, (t, tp + t), 0)
    col = lax.broadcasted_iota(jnp.int32, (t, tp + t), 1)
    no_prev = jnp.where(i > 0, 0, reach + 1 + tp + t)
    dist = row + tp - col + jnp.where(col < tp, no_prev, 0)
    valid = dist.astype(jnp.uint32) <= reach
    lane = lax.broadcasted_iota(jnp.int32, (t, LANES), 1)
    lse_tile = jnp.zeros((t, LANES), F32)
    heads = [slice(h * LANES, (h + 1) * LANES) for h in range(n_heads)]
    scores = [_nt_dot(q_ref[0, 0, :, sl], jnp.concatenate([kp_ref[0, 0, :, sl], kc_ref[0, 0, :, sl]], axis=0))
              for sl in heads]
    for h, sl in enumerate(heads):
        v = jnp.concatenate([vp_ref[0, 0, :, sl], vc_ref[0, 0, :, sl]], axis=0)
        s = jnp.where(valid, scores[h], NEG_INF)
        m = jnp.max(s, axis=-1, keepdims=True)
        p = jnp.exp2(s - m)
        l = jnp.sum(p, axis=-1, keepdims=True)
        o_ref[0, 0, :, sl] = (_dot(p.astype(BF16), v) * (1.0 / l)).astype(o_ref.dtype)
        lse_tile = jnp.where(lane == h, m + jnp.log2(l), lse_tile)
    lse_ref[0, 0] = lse_tile


def _dil_attn(qkv, *, reach, n_heads, t, name):
    batch, dil, su, _ = qkv.shape
    w = n_heads * LANES
    tp = reach
    assert t % tp == 0 and tp % 16 == 0
    cur = lambda part: (lambda b, r, i: (b, r, i, part))
    prev = lambda part: (lambda b, r, i: (b, r, jnp.maximum(i * (t // tp) - 1, 0), part))
    return pl.pallas_call(
        functools.partial(_dil_attn_kernel, reach=reach, n_heads=n_heads),
        grid=(batch, dil, su // t),
        in_specs=[pl.BlockSpec((1, 1, t, w), cur(0)),
                  pl.BlockSpec((1, 1, tp, w), prev(1)),
                  pl.BlockSpec((1, 1, t, w), cur(1)),
                  pl.BlockSpec((1, 1, tp, w), prev(2)),
                  pl.BlockSpec((1, 1, t, w), cur(2))],
        out_specs=[pl.BlockSpec((1, 1, t, w), lambda b, r, i: (b, r, i, 0)),
                   pl.BlockSpec((1, 1, t, LANES), lambda b, r, i: (b, r, i, 0))],
        out_shape=[jax.ShapeDtypeStruct((batch, dil, su, w), BF16),
                   jax.ShapeDtypeStruct((batch, dil, su, LANES), F32)],
        compiler_params=_cparams(3, 40),
        name=name,
    )(qkv, qkv, qkv, qkv, qkv)


def _mix_out_kernel(*refs, dils, n_heads):
    n_g = len(dils)
    o_refs = refs[:n_g]
    lse_refs = refs[n_g:2 * n_g]
    w_ref, r_ref, out_ref, oc_ref, uo_ref, ul_ref = refs[2 * n_g:]
    tm = oc_ref.shape[0]
    for g, d in enumerate(dils):
        for r in range(d):
            rows = pl.ds(r, tm // d, stride=d)
            for h in range(n_heads):
                uo_ref[g, h, rows, :] = o_refs[g][0, r, :, h * LANES:(h + 1) * LANES].astype(F32)
            ul_ref[g, rows, :] = lse_refs[g][0, r]
    m = ul_ref[0]
    for g in range(1, n_g):
        m = jnp.maximum(m, ul_ref[g])
    e = [jnp.exp2(ul_ref[g] - m) for g in range(n_g)]
    den = e[0]
    for g in range(1, n_g):
        den = den + e[g]
    inv = 1.0 / den
    for h in range(n_heads):
        sl = slice(h * LANES, (h + 1) * LANES)
        acc = (e[0] * inv)[:, h:h + 1] * uo_ref[0, h]
        for g in range(1, n_g):
            acc = acc + (e[g] * inv)[:, h:h + 1] * uo_ref[g, h]
        oc_ref[:, sl] = acc.astype(BF16)
    out_ref[...] = r_ref[...] + _dot(oc_ref[...], w_ref[...])


def _mix_out(os_, lses, w, r, *, seq, n_heads, tm=512, name):
    m, n = r.shape
    k = w.shape[0]
    dils = tuple(o.shape[1] for o in os_)
    nsb = seq // tm
    cls = lambda i: (i // nsb, 0, i % nsb, 0)
    return pl.pallas_call(
        functools.partial(_mix_out_kernel, dils=dils, n_heads=n_heads),
        grid=(m // tm,),
        in_specs=([pl.BlockSpec((1, d, tm // d, k), cls) for d in dils]
                  + [pl.BlockSpec((1, d, tm // d, LANES), cls) for d in dils]
                  + [pl.BlockSpec((k, n), lambda i: (0, 0)),
                     pl.BlockSpec((tm, n), lambda i: (i, 0))]),
        out_specs=pl.BlockSpec((tm, n), lambda i: (i, 0)),
        out_shape=jax.ShapeDtypeStruct((m, n), F32),
        scratch_shapes=[pltpu.VMEM((tm, k), BF16), pltpu.VMEM((len(dils), n_heads, tm, LANES), F32),
                        pltpu.VMEM((len(dils), tm, LANES), F32)],
        compiler_params=_cparams(1, 56),
        name=name,
    )(*os_, *lses, w, r)


def _cmp_kernel(z_ref, w1_ref, w2_ref, pe_ref, tab_ref, o_ref, *, n_cmp):
    w1 = w1_ref[0]
    z = z_ref[0, 0]
    nc, half = z.shape
    bias = _dot(pe_ref[0].astype(BF16), w1)[0:1]
    hid = _dot(z, w1[:half]) + pltpu.roll(_dot(z, w1[half:]), nc - 1, 0) + bias
    hid = 0.5 * hid * (1.0 + jnp.tanh(math.sqrt(2.0 / math.pi) * (hid + 0.044715 * (hid * hid * hid))))
    o = _apply_rope(_dot(hid.astype(BF16), w2_ref[0]), tab_ref[0, 0], tab_ref[0, 1])
    row = lax.broadcasted_iota(jnp.int32, o.shape, 0)
    o_ref[0, 0] = jnp.where(row < n_cmp, o, 0.0).astype(o_ref.dtype)


def _compress(z, w1, w2, pe, tab, *, n_cmp, name):
    two, bg, nc, half = z.shape
    ld = w1.shape[1]
    hid = w1.shape[2]
    return pl.pallas_call(
        functools.partial(_cmp_kernel, n_cmp=n_cmp),
        grid=(two, bg),
        in_specs=[pl.BlockSpec((1, 1, nc, half), lambda s, b: (s, b, 0, 0)),
                  pl.BlockSpec((1, ld, hid), lambda s, b: (s, 0, 0)),
                  pl.BlockSpec((1, hid, LANES), lambda s, b: (s, 0, 0)),
                  pl.BlockSpec((1, SUBLANES, ld), lambda s, b: (s, 0, 0)),
                  pl.BlockSpec((1, 2, nc, LANES), lambda s, b: (s, 0, 0, 0))],
        out_specs=pl.BlockSpec((1, 1, nc, LANES), lambda s, b: (s, b, 0, 0)),
        out_shape=jax.ShapeDtypeStruct((two, bg, nc, LANES), BF16),
        compiler_params=_cparams(2, 40),
        name=name,
    )(z, w1, w2, pe, tab)


def _nsa_kernel(q_ref, gate_ref, kc_ref, vct_ref, ks_ref, vst_ref, kw_ref, vwt_ref, ovt_ref, o_ref, kaug_ref,
                acc_ref, *, tq, n_sub, tk, hpg, top_n):
    qi = pl.program_id(2)
    r = hpg * tq
    seq = ks_ref.shape[1]
    nj = ovt_ref.shape[0]
    nc = kc_ref.shape[2]

    @pl.when(qi == 0)
    def _():
        def fill(c, carry):
            r0 = pl.multiple_of(c * tk, tk)
            kaug_ref[pl.ds(r0, tk), 0:LANES] = ks_ref[0, pl.ds(r0, tk), :]
            blk = (r0 + lax.broadcasted_iota(jnp.int32, (tk, nj), 0)) // SLC_LEN
            hot = jnp.where(blk == lax.broadcasted_iota(jnp.int32, (tk, nj), 1), 1.0, 0.0)
            kaug_ref[pl.ds(r0, tk), LANES:LANES + nj] = hot.astype(BF16)
            return carry
        lax.fori_loop(0, seq // tk, fill, 0)

    subs = range(n_sub)
    n_wt = WIN_LEN // tq + 1
    span = n_wt * tq
    t0 = [(qi * n_sub + u) * tq for u in subs]
    qs, tcol = [], []
    for u in subs:
        q = q_ref[0, u * tq:(u + 1) * tq, :]
        qs.append(jnp.concatenate([q[:, h * LANES:(h + 1) * LANES] for h in range(hpg)], axis=0))
        tcol.append(t0[u] + (lax.broadcasted_iota(jnp.int32, (1, r), 1) & (tq - 1)))

    s_c = [_nt_dot(kc_ref[0, 0], qs[u]) for u in subs]
    start = [pl.multiple_of(jnp.maximum(t0[u] - WIN_LEN, 0), tq) for u in subs]
    s_w = [_nt_dot(kw_ref[0, pl.ds(start[u], span), :], qs[u]) for u in subs]

    cend = lax.broadcasted_iota(jnp.int32, (nc, 1), 0) * CMP_STRIDE + (CMP_LEN - 1)
    ovt = ovt_ref[...]
    o_cmp, imp = [], []
    for u in subs:
        sc = jnp.where(cend <= tcol[u], s_c[u], NEG_INF)
        m_c = jnp.max(sc, axis=0, keepdims=True)
        e_c = jnp.exp2(sc - m_c)
        inv_c = jnp.where(m_c > 0.5 * NEG_INF, 1.0 / jnp.maximum(jnp.sum(e_c, axis=0, keepdims=True), TINY), 0.0)
        p_c = e_c * inv_c
        o_cmp.append(_dot(vct_ref[0, 0], p_c.astype(BF16)))
        psum = p_c[:, 0:tq]
        for h in range(1, hpg):
            psum = psum + p_c[:, h * tq:(h + 1) * tq]
        p_hi = psum.astype(BF16)
        rem = psum - p_hi.astype(F32)
        p_mid = rem.astype(BF16)
        p_lo = (rem - p_mid.astype(F32)).astype(BF16)
        imp.append(_dot(ovt, p_hi) + _dot(ovt, p_mid) + _dot(ovt, p_lo))

    acc_w = []
    for u in subs:
        dist = tcol[u] - (start[u] + lax.broadcasted_iota(jnp.int32, (span, 1), 0))
        sw = jnp.where(dist.astype(jnp.uint32) < WIN_LEN, s_w[u], NEG_INF)
        p_w = jnp.exp2(sw - jnp.max(sw, axis=0, keepdims=True)).astype(BF16)
        wt0 = start[u] // tq
        vw = jnp.concatenate([vwt_ref[0, 0, wt0 + i] for i in range(n_wt)], axis=1)
        acc_w.append(_dot(vw, p_w))

    jrow = lax.broadcasted_iota(jnp.int32, (nj, tq), 0)
    score = []
    for u in subs:
        tok = t0[u] + lax.broadcasted_iota(jnp.int32, (nj, tq), 1)
        cur = tok // SLC_LEN
        forced = (jrow == 0) | (jrow == cur) | (jrow == cur - 1)
        score.append(jnp.where(forced, FORCE_SCORE, jnp.where(jrow * SLC_LEN <= tok, imp[u], NEG_INF)))
    for _ in range(top_n):
        for u in subs:
            mx = jnp.max(score[u], axis=0, keepdims=True)
            first = jnp.min(jnp.where(score[u] == mx, jrow, nj), axis=0, keepdims=True)
            score[u] = jnp.where(jrow == first, REMOVED_SCORE, score[u])
    q_aug = []
    for u in subs:
        bias = jnp.where(score[u] < 0.5 * REMOVED_SCORE, 0.0, NEG_INF).T.astype(BF16)
        q_aug.append(jnp.concatenate([qs[u], jnp.concatenate([bias] * hpg, axis=0)], axis=1))

    def slc_tile(kj, m_run, diagonal):
        k0 = pl.multiple_of(kj * tk, tk)
        ka = kaug_ref[pl.ds(k0, tk), :]
        vt = vst_ref[0, 0, kj]
        out = []
        scores = [_nt_dot(ka, q_aug[u]) for u in subs]
        for u in subs:
            s = scores[u]
            if diagonal:
                s = jnp.where(k0 + lax.broadcasted_iota(jnp.int32, (tk, 1), 0) <= tcol[u], s, NEG_INF)
            m_new = jnp.maximum(m_run[u], jnp.max(s, axis=0, keepdims=True))
            p = jnp.exp2(s - m_new).astype(BF16)
            acc_ref[u] = jnp.exp2(m_run[u] - m_new) * acc_ref[u] + _dot(vt, p)
            out.append(m_new)
        return tuple(out)

    n_full = (qi * n_sub * tq) // tk
    for u in subs:
        acc_ref[u] = jnp.zeros((LANES + SUBLANES, r), F32)
    m_run = (jnp.full((1, r), NEG_INF, F32),) * n_sub
    m_run = lax.fori_loop(0, n_full, lambda kj, c: slc_tile(kj, c, False), m_run)
    slc_tile(n_full, m_run, True)

    for u in subs:
        acc_s = acc_ref[u]
        g_t = gate_ref[0, u * tq:(u + 1) * tq, :].T

        def gate_row(br):
            return jnp.concatenate([g_t[br * hpg + h:br * hpg + h + 1, :] for h in range(hpg)], axis=1)

        o_t = (gate_row(0) * o_cmp[u]
               + (gate_row(1) * (1.0 / acc_s[LANES:LANES + 1])) * acc_s[0:LANES]
               + (gate_row(2) * (1.0 / acc_w[u][LANES:LANES + 1])) * acc_w[u][0:LANES])
        for h in range(hpg):
            o_ref[0, u * tq:(u + 1) * tq, h * LANES:(h + 1) * LANES] = (
                o_t[:, h * tq:(h + 1) * tq].T.astype(o_ref.dtype))


def _nsa(q, gates, kv, kc, vct, vst, vwt, overlap_t, *, n_groups, hpg, top_n, tq, n_sub, tk, name):
    b, s, _ = q.shape
    nc = kc.shape[2]
    nj = overlap_t.shape[0]
    g_ = n_groups
    seg = lambda sidx: (lambda bi, gi, qi: (bi, 0, sidx * g_ + gi))
    bg = lambda bi, gi, qi: (bi * g_ + gi, 0, 0, 0)
    vrows = LANES + SUBLANES
    tqs = n_sub * tq
    assert tk % tqs == 0 and s % tk == 0
    return pl.pallas_call(
        functools.partial(_nsa_kernel, tq=tq, n_sub=n_sub, tk=tk, hpg=hpg, top_n=top_n),
        grid=(b, g_, s // tqs),
        in_specs=[pl.BlockSpec((1, tqs, hpg * LANES), lambda bi, gi, qi: (bi, qi, gi)),
                  pl.BlockSpec((1, tqs, LANES), lambda bi, gi, qi: (bi, qi, gi)),
                  pl.BlockSpec((1, 1, nc, LANES), bg),
                  pl.BlockSpec((1, 1, LANES, nc), bg),
                  pl.BlockSpec((1, s, LANES), seg(KV_K_SLC)),
                  pl.BlockSpec((1, 1, s // tk, vrows, tk), lambda bi, gi, qi: (bi, gi, 0, 0, 0)),
                  pl.BlockSpec((1, s, LANES), seg(KV_K_WIN)),
                  pl.BlockSpec((1, 1, s // tq, vrows, tq), lambda bi, gi, qi: (bi, gi, 0, 0, 0)),
                  pl.BlockSpec((nj, nc), lambda bi, gi, qi: (0, 0))],
        out_specs=pl.BlockSpec((1, tqs, hpg * LANES), lambda bi, gi, qi: (bi, qi, gi)),
        out_shape=jax.ShapeDtypeStruct(q.shape, BF16),
        scratch_shapes=[pltpu.VMEM((s, LANES + nj), BF16),
                        pltpu.VMEM((n_sub, vrows, hpg * tq), F32)],
        compiler_params=_cparams(3, 48),
        name=name,
    )(q, gates, kc, vct, kv, vst, kv, vwt, overlap_t)


KV_SEGS = (2, 4, 0, 1, 3, 5)
KV_ROPE_SRC = (2, 4)
KV_K_SLC, KV_K_WIN, KV_K_CMP, KV_V_CMP, KV_V_SLC, KV_V_WIN = range(6)

NSA_TQ = 128
NSA_SUB = 4
NSA_TK = 512


def _dilated_layer(h, g_attn, w_in, w_out, tables, batch, seq):
    m, d = h.shape
    ng = len(DILATED_GROUPS)
    dils = tuple(dl for _, dl in DILATED_GROUPS)
    hg = w_out.shape[0] // HEAD_DIM
    wd = hg * HEAD_DIM
    reach = DILATED_GROUPS[0][0] // DILATED_GROUPS[0][1]
    assert all(win // dl == reach for win, dl in DILATED_GROUPS)
    w4 = w_in.reshape(d, 3, ng, wd)
    ws = [jnp.concatenate([_perm_head_cols(w4[:, 0, gi]), _perm_head_cols(w4[:, 1, gi]), w4[:, 2, gi]],
                          axis=1).astype(BF16) for gi in range(ng)]
    tm = min(512, seq)
    os_, lses = [], []
    for gi, dil in enumerate(dils):
        qkv = _a_qkv(h, g_attn, ws[gi], tables, batch=batch, seq=seq, dil=dil,
                     part_width=wd, tm=tm, name=f"a_qkv{gi}")
        o, lse = _dil_attn(qkv, reach=reach, n_heads=hg, t=min(256, seq // dil), name=f"a_attn{gi}")
        os_.append(o)
        lses.append(lse)
    return _mix_out(os_, lses, w_out.astype(BF16), h, seq=seq, n_heads=hg, tm=tm, name="a_out")


def _value_tiles_t(v, batch, seq, g_, tile):
    vt = v.reshape(batch, seq // tile, tile, g_, HEAD_DIM).transpose(0, 3, 1, 4, 2)
    ones = jnp.ones((batch, g_, seq // tile, SUBLANES, tile), BF16)
    return jnp.concatenate([vt, ones], axis=3)


def _nsa_shared_kv(h, kv_norm_g, w_kv, cmp_k, cmp_v, tables, batch, seq):
    g_ = NSA_KV_GROUPS
    seg_w = g_ * HEAD_DIM
    (pe_k, w1_k, w2_k), (pe_v, w1_v, w2_v) = cmp_k, cmp_v
    src = lambda s: w_kv[:, s * seg_w:(s + 1) * seg_w]
    w = jnp.concatenate([_perm_head_cols(src(s)) if s in KV_ROPE_SRC else src(s) for s in KV_SEGS],
                        axis=1).astype(BF16)
    kv = _norm_matmul(h, kv_norm_g, w, out_dtype=BF16, tables=tables, seq=seq,
                      tab_fn=lambda j: jnp.where(j == 0, ROPE_K, ROPE_NONE), tn=len(KV_ROPE_SRC) * seg_w,
                      name="b_kv")
    n_chunk = seq // CMP_STRIDE
    n_cmp = (seq - CMP_LEN) // CMP_STRIDE + 1
    assert CMP_LEN == 2 * CMP_STRIDE

    assert KV_V_CMP == KV_K_CMP + 1
    z = kv[:, KV_K_CMP * seg_w:(KV_V_CMP + 1) * seg_w].reshape(batch, n_chunk, CMP_STRIDE, 2, g_, HEAD_DIM)
    z = z.transpose(3, 0, 4, 1, 2, 5).reshape(2, batch * g_, n_chunk, CMP_STRIDE * HEAD_DIM)
    pe = jnp.stack([pe_k.reshape(1, -1), pe_v.reshape(1, -1)])
    pe = jnp.broadcast_to(pe, (2, SUBLANES, pe.shape[-1]))
    cmp_tab = _rope_tables(jnp.arange(n_chunk) * CMP_STRIDE + CMP_LEN - 1, (1.0,))
    cmp_kv = _compress(z, jnp.stack([w1_k, w1_v]).astype(BF16),
                       jnp.stack([_perm_head_cols(w2_k), w2_v]).astype(BF16),
                       pe, cmp_tab, n_cmp=n_cmp, name="b_compress")
    kc = cmp_kv[0][:, None]
    vct = cmp_kv[1].transpose(0, 2, 1)[:, None]
    vst = _value_tiles_t(kv[:, KV_V_SLC * seg_w:(KV_V_SLC + 1) * seg_w], batch, seq, g_, min(NSA_TK, seq))
    vwt = _value_tiles_t(kv[:, KV_V_WIN * seg_w:(KV_V_WIN + 1) * seg_w], batch, seq, g_, NSA_TQ)
    return kv, kc, vct, vst, vwt


def _nsa_layer(h, g_attn, w_qg, w_out, shared, tables, batch, seq):
    m, d = h.shape
    g_ = NSA_KV_GROUPS
    n_heads = w_out.shape[0] // HEAD_DIM
    hpg = n_heads // g_
    seg_w = g_ * HEAD_DIM
    kv, kc, vct, vst, vwt = shared
    n_chunk = seq // CMP_STRIDE
    n_cmp = (seq - CMP_LEN) // CMP_STRIDE + 1
    n_q = n_heads * HEAD_DIM
    q = _norm_matmul(h, g_attn, _perm_head_cols(w_qg[:, :n_q]).astype(BF16), out_dtype=BF16,
                     tables=tables, seq=seq, tab_fn=lambda j: ROPE_Q, tn=1024,
                     name="b_q")
    wg = w_qg[:, n_q:].reshape(d, N_NSA_BRANCH, g_, hpg).transpose(0, 2, 1, 3).reshape(d, g_, N_NSA_BRANCH * hpg)
    wg = jnp.pad(wg, ((0, 0), (0, 0), (0, LANES - N_NSA_BRANCH * hpg))).reshape(d, g_ * LANES)
    gates = _norm_matmul(h, g_attn, wg.astype(BF16), out_dtype=F32, act="sigmoid", tn=g_ * LANES,
                         name="b_gates")

    n_slc = seq // SLC_LEN
    nj = -(-n_slc // LANES) * LANES
    cs = np.arange(n_chunk)[:, None] * CMP_STRIDE
    ss = np.arange(nj)[None, :] * SLC_LEN
    ov = np.clip(np.minimum(cs + CMP_LEN, ss + SLC_LEN) - np.maximum(cs, ss), 0, None) / CMP_LEN
    ov[n_cmp:, :] = 0.0
    overlap_t = jnp.asarray(ov.T, BF16)

    o = _nsa(q.reshape(batch, seq, n_q), gates.reshape(batch, seq, g_ * LANES),
             kv.reshape(batch, seq, 6 * seg_w), kc, vct, vst, vwt, overlap_t,
             n_groups=g_, hpg=hpg, top_n=min(SLC_TOP_N, n_slc), tq=NSA_TQ, n_sub=NSA_SUB, tk=min(NSA_TK, seq),
             name="b_nsa")
    return _matmul_resid(o.reshape(m, n_q), w_out.astype(BF16), h, name="b_out")


def kernel(x, p, a_w_in, a_w_out, b_w_qg, b_w_out, kv_norm_g, w_kv_shared, cmp_pe_k, cmp_w1_k, cmp_w2_k,
           cmp_pe_v, cmp_w1_v, cmp_w2_v, attn_norm_g, mlp_norm_g, mlp_w1, mlp_w2, ple_norm_g, ple_w_gate,
           ple_w_proj, final_norm_g):
    batch, seq, d = x.shape
    m = batch * seq
    depth = attn_norm_g.shape[0]
    n_a = a_w_in.shape[0]
    h = x.reshape(m, d)
    tables = _seq_rope_tables(seq)
    shared = None
    for i in range(depth):
        if i < n_a:
            h = _dilated_layer(h, attn_norm_g[i], a_w_in[i], a_w_out[i], tables, batch, seq)
        else:
            if i == n_a:
                shared = _nsa_shared_kv(h, kv_norm_g, w_kv_shared, (cmp_pe_k, cmp_w1_k, cmp_w2_k),
                                        (cmp_pe_v, cmp_w1_v, cmp_w2_v), tables, batch, seq)
            j = i - n_a
            h = _nsa_layer(h, attn_norm_g[i], b_w_qg[j], b_w_out[j], shared, tables, batch, seq)
        h = _mlp(h, mlp_norm_g[i], mlp_w1[i].astype(BF16), mlp_w2[i].astype(BF16), name=f"mlp{i}")
        h = _ple(h, ple_norm_g[i], ple_w_gate[i].astype(BF16), p[i].reshape(m, -1),
                 ple_w_proj[i].astype(BF16), final_g=final_norm_g if i == depth - 1 else None, name=f"ple{i}")
    if depth == 0:
        h = _rmsnorm(h, final_norm_g, name="final_norm")
    return h.reshape(batch, seq, d)
```

```python
import functools
import math

import jax
import jax.numpy as jnp
import numpy as np
from jax import lax
from jax.experimental import pallas as pl
from jax.experimental.pallas import tpu as pltpu

F32 = jnp.float32
BF16 = jnp.bfloat16

HEAD_DIM = 128
ROT_DIM = HEAD_DIM // 4
ROPE_THETA = 500000.0
NORM_EPS = 1e-6
NEG_INF = -1e30
FORCE_SCORE = 1e9
TINY = 1e-20
REMOVED_SCORE = -3.0e38
LOG2E = math.log2(math.e)

DILATED_GROUPS = ((128, 1), (512, 4), (2048, 16))
NSA_KV_GROUPS = 4
CMP_LEN = 32
CMP_STRIDE = 16
SLC_LEN = 64
SLC_TOP_N = 16
WIN_LEN = 512
N_NSA_BRANCH = 3

LANES = 128
SUBLANES = 8
MIB = 1024 * 1024


def _cparams(n_grid, vmem_mib):
    return pltpu.CompilerParams(dimension_semantics=("arbitrary",) * n_grid,
                                vmem_limit_bytes=vmem_mib * MIB)


def _nt_dot(a, b):
    return lax.dot_general(a, b, (((1,), (1,)), ((), ())), preferred_element_type=F32)


def _dot(a, b):
    return jnp.dot(a, b, preferred_element_type=F32)


def _perm_head_cols(w):
    k, n = w.shape
    half = ROT_DIM // 2
    w = w.reshape(k, n // HEAD_DIM, HEAD_DIM)
    w = jnp.concatenate([w[..., :half], w[..., ROT_DIM:LANES // 2 + half], w[..., half:ROT_DIM],
                         w[..., LANES // 2 + half:]], axis=-1)
    return w.reshape(k, n)


def _rope_tables(pos, scales):
    n = pos.shape[0]
    half = ROT_DIM // 2
    inv = ROPE_THETA ** (-jnp.arange(half, dtype=F32) * (2.0 / ROT_DIM))
    ang = pos.astype(F32)[:, None] * inv[None, :]
    c, s = jnp.cos(ang), jnp.sin(ang)
    ones = jnp.ones((n, LANES // 2 - half), F32)
    zeros = jnp.zeros((n, LANES // 2 - half), F32)
    rot = jnp.stack([jnp.concatenate([c, ones, c, ones], axis=1),
                     jnp.concatenate([-s, zeros, s, zeros], axis=1)])
    ident = jnp.stack([jnp.ones((n, LANES), F32), jnp.zeros((n, LANES), F32)])
    return jnp.stack([rot * sc for sc in scales] + [ident])


ROPE_Q, ROPE_K, ROPE_NONE = range(3)


def _seq_rope_tables(seq):
    return _rope_tables(jnp.arange(seq), (HEAD_DIM ** -0.5 * LOG2E, 1.0))


def _apply_rope(a, cos, sg):
    return a * cos + pltpu.roll(a, LANES // 2, 1) * sg


def _rms_scale(x, g):
    ms = jnp.mean(x * x, axis=-1, keepdims=True)
    return (x * lax.rsqrt(ms + NORM_EPS) * g).astype(BF16)


def _norm_matmul_kernel(x_ref, g_ref, w_ref, *rest, rope, act):
    if rope:
        tab_ref, o_ref, xn_ref = rest
    else:
        o_ref, xn_ref = rest

    @pl.when(pl.program_id(1) == 0)
    def _():
        xn_ref[...] = _rms_scale(x_ref[...], g_ref[...])

    acc = _dot(xn_ref[...], w_ref[...])
    if rope:
        cos, sg = tab_ref[0, 0], tab_ref[0, 1]
        for c in range(acc.shape[1] // LANES):
            sl = slice(c * LANES, (c + 1) * LANES)
            o_ref[:, sl] = _apply_rope(acc[:, sl], cos, sg).astype(o_ref.dtype)
    else:
        if act == "sigmoid":
            acc = 1.0 / (1.0 + jnp.exp(-acc))
        o_ref[...] = acc.astype(o_ref.dtype)


def _norm_matmul(x, g, w, *, out_dtype, tables=None, tab_fn=None, seq=None, act=None,
                 tm=512, tn=512, name):
    m, k = x.shape
    n = w.shape[1]
    tm, tn = min(tm, m), min(tn, n)
    in_specs = [pl.BlockSpec((tm, k), lambda i, j: (i, 0)),
                pl.BlockSpec((1, k), lambda i, j: (0, 0)),
                pl.BlockSpec((k, tn), lambda i, j: (0, j))]
    args = [x, g.reshape(1, k), w]
    if tables is not None:
        nsb = seq // tm
        in_specs.append(pl.BlockSpec((1, 2, tm, LANES), lambda i, j: (tab_fn(j), 0, i % nsb, 0)))
        args.append(tables)
    return pl.pallas_call(
        functools.partial(_norm_matmul_kernel, rope=tables is not None, act=act),
        grid=(m // tm, n // tn),
        in_specs=in_specs,
        out_specs=pl.BlockSpec((tm, tn), lambda i, j: (i, j)),
        out_shape=jax.ShapeDtypeStruct((m, n), out_dtype),
        scratch_shapes=[pltpu.VMEM((tm, k), BF16)],
        compiler_params=_cparams(2, 40),
        name=name,
    )(*args)


def _matmul_resid_kernel(a_ref, w_ref, r_ref, o_ref):
    o_ref[...] = r_ref[...] + _dot(a_ref[...], w_ref[...])


def _matmul_resid(a, w, r, *, tm=512, tn=1024, name):
    m, k = a.shape
    n = w.shape[1]
    tm, tn = min(tm, m), min(tn, n)
    return pl.pallas_call(
        _matmul_resid_kernel,
        grid=(m // tm, n // tn),
        in_specs=[pl.BlockSpec((tm, k), lambda i, j: (i, 0)),
                  pl.BlockSpec((k, tn), lambda i, j: (0, j)),
                  pl.BlockSpec((tm, tn), lambda i, j: (i, j))],
        out_specs=pl.BlockSpec((tm, tn), lambda i, j: (i, j)),
        out_shape=jax.ShapeDtypeStruct((m, n), F32),
        compiler_params=_cparams(2, 40),
        name=name,
    )(a, w, r)


def _mlp_kernel(x_ref, g_ref, w1_ref, w2_ref, o_ref, xn_ref, acc_ref):
    f = pl.program_id(1)

    @pl.when(f == 0)
    def _():
        xn_ref[...] = _rms_scale(x_ref[...], g_ref[...])
        acc_ref[...] = jnp.zeros_like(acc_ref)

    a = _dot(xn_ref[...], w1_ref[...])
    a = jnp.square(jnp.maximum(a, 0.0)).astype(BF16)
    acc_ref[...] += _dot(a, w2_ref[...])

    @pl.when(f == pl.num_programs(1) - 1)
    def _():
        o_ref[...] = x_ref[...] + acc_ref[...]


def _mlp(h, g, w1, w2, *, tm=1024, tf=512, name):
    m, d = h.shape
    ff = w1.shape[1]
    tm, tf = min(tm, m), min(tf, ff)
    return pl.pallas_call(
        _mlp_kernel,
        grid=(m // tm, ff // tf),
        in_specs=[pl.BlockSpec((tm, d), lambda i, f: (i, 0)),
                  pl.BlockSpec((1, d), lambda i, f: (0, 0)),
                  pl.BlockSpec((d, tf), lambda i, f: (0, f)),
                  pl.BlockSpec((tf, d), lambda i, f: (f, 0))],
        out_specs=pl.BlockSpec((tm, d), lambda i, f: (i, 0)),
        out_shape=jax.ShapeDtypeStruct((m, d), F32),
        scratch_shapes=[pltpu.VMEM((tm, d), BF16), pltpu.VMEM((tm, d), F32)],
        compiler_params=_cparams(2, 56),
        name=name,
    )(h, g.reshape(1, d), w1, w2)


PLE_COL_CHUNK = 1024


def _ple_kernel(x_ref, g_ref, wg_ref, p_ref, wp_ref, *rest, final):
    if final:
        fg_ref, o_ref = rest
    else:
        (o_ref,) = rest
    n = o_ref.shape[1]
    xn = _rms_scale(x_ref[...], g_ref[...])
    pb = p_ref[...].astype(BF16)
    ssq = None
    for c0 in range(0, n, PLE_COL_CHUNK):
        sl = slice(c0, c0 + PLE_COL_CHUNK)
        gate = 1.0 / (1.0 + jnp.exp(-_dot(xn, wg_ref[:, sl])))
        hc = x_ref[:, sl] + _dot(pb, wp_ref[:, sl]) * gate
        o_ref[:, sl] = hc
        if final:
            part = jnp.sum(hc * hc, axis=-1, keepdims=True)
            ssq = part if ssq is None else ssq + part
    if final:
        o_ref[...] = o_ref[...] * lax.rsqrt(ssq * (1.0 / n) + NORM_EPS) * fg_ref[...]


def _ple(h, g, wg, p, wp, *, final_g=None, tm=512, name):
    m, d = h.shape
    dp = p.shape[1]
    n = wg.shape[1]
    assert n == d and n % PLE_COL_CHUNK == 0
    tm = min(tm, m)
    row = lambda i: (i, 0)
    const = lambda i: (0, 0)
    in_specs = [pl.BlockSpec((tm, d), row), pl.BlockSpec((1, d), const), pl.BlockSpec((d, n), const),
                pl.BlockSpec((tm, dp), row), pl.BlockSpec((dp, n), const)]
    args = [h, g.reshape(1, d), wg, p, wp]
    if final_g is not None:
        in_specs.append(pl.BlockSpec((1, n), const))
        args.append(final_g.reshape(1, n))
    return pl.pallas_call(
        functools.partial(_ple_kernel, final=final_g is not None),
        grid=(m // tm,),
        in_specs=in_specs,
        out_specs=pl.BlockSpec((tm, n), row),
        out_shape=jax.ShapeDtypeStruct((m, n), F32),
        compiler_params=_cparams(1, 56),
        name=name,
    )(*args)


def _rmsnorm_kernel(x_ref, g_ref, o_ref):
    x = x_ref[...]
    ms = jnp.mean(x * x, axis=-1, keepdims=True)
    o_ref[...] = x * lax.rsqrt(ms + NORM_EPS) * g_ref[...]


def _rmsnorm(h, g, *, tm=512, name):
    m, d = h.shape
    tm = min(tm, m)
    return pl.pallas_call(
        _rmsnorm_kernel,
        grid=(m // tm,),
        in_specs=[pl.BlockSpec((tm, d), lambda i: (i, 0)),
                  pl.BlockSpec((1, d), lambda i: (0, 0))],
        out_specs=pl.BlockSpec((tm, d), lambda i: (i, 0)),
        out_shape=jax.ShapeDtypeStruct((m, d), F32),
        compiler_params=_cparams(1, 32),
        name=name,
    )(h, g.reshape(1, d))


def _a_qkv_kernel(x_ref, g_ref, w_ref, tab_ref, o_ref, xn_ref, res_ref, *, dil):
    @pl.when(pl.program_id(1) == 0)
    def _():
        xn_ref[...] = _rms_scale(x_ref[...], g_ref[...])

    acc = _dot(xn_ref[...], w_ref[...])
    tm, tn = acc.shape
    cos, sg = tab_ref[0, 0], tab_ref[0, 1]
    for c in range(tn // LANES):
        sl = slice(c * LANES, (c + 1) * LANES)
        res = _apply_rope(acc[:, sl], cos, sg)
        if dil == 1:
            o_ref[0, 0, :, sl] = res.astype(BF16)
        else:
            res_ref[c] = res
            for r in range(dil):
                o_ref[0, r, :, sl] = res_ref[c, pl.ds(r, tm // dil, stride=dil), :].astype(BF16)


def _a_qkv(x, g, w, tables, *, batch, seq, dil, part_width, tm=512, tn=1024, name):
    m, k = x.shape
    n = w.shape[1]
    nsb = seq // tm
    bpp = part_width // tn
    assert tm % (dil * 16) == 0
    return pl.pallas_call(
        functools.partial(_a_qkv_kernel, dil=dil),
        grid=(m // tm, n // tn),
        in_specs=[pl.BlockSpec((tm, k), lambda i, j: (i, 0)),
                  pl.BlockSpec((1, k), lambda i, j: (0, 0)),
                  pl.BlockSpec((k, tn), lambda i, j: (0, j)),
                  pl.BlockSpec((1, 2, tm, LANES), lambda i, j: (j // bpp, 0, i % nsb, 0))],
        out_specs=pl.BlockSpec((1, dil, tm // dil, tn), lambda i, j: (i // nsb, 0, i % nsb, j)),
        out_shape=jax.ShapeDtypeStruct((batch, dil, seq // dil, n), BF16),
        scratch_shapes=[pltpu.VMEM((tm, k), BF16), pltpu.VMEM((tn // LANES, tm, LANES), F32)],
        compiler_params=_cparams(2, 40),
        name=name,
    )(x, g.reshape(1, k), w, tables)


def _dil_attn_kernel(q_ref, kp_ref, kc_ref, vp_ref, vc_ref, o_ref, lse_ref, *, reach, n_heads):
    i = pl.program_id(2)
    t = q_ref.shape[2]
    tp = kp_ref.shape[2]
    row = lax.broadcasted_iota(jnp.int32, (t, tp + t), 0)
    col = lax.broadcasted_iota(jnp.int32, (t, tp + t), 1)
    no_prev = jnp.where(i > 0, 0, reach + 1 + tp + t)
    dist = row + tp - col + jnp.where(col < tp, no_prev, 0)
    valid = dist.astype(jnp.uint32) <= reach
    lane = lax.broadcasted_iota(jnp.int32, (t, LANES), 1)
    lse_tile = jnp.zeros((t, LANES), F32)
    heads = [slice(h * LANES, (h + 1) * LANES) for h in range(n_heads)]
    scores = [_nt_dot(q_ref[0, 0, :, sl], jnp.concatenate([kp_ref[0, 0, :, sl], kc_ref[0, 0, :, sl]], axis=0))
              for sl in heads]
    for h, sl in enumerate(heads):
        v = jnp.concatenate([vp_ref[0, 0, :, sl], vc_ref[0, 0, :, sl]], axis=0)
        s = jnp.where(valid, scores[h], NEG_INF)
        m = jnp.max(s, axis=-1, keepdims=True)
        p = jnp.exp2(s - m)
        l = jnp.sum(p, axis=-1, keepdims=True)
        o_ref[0, 0, :, sl] = (_dot(p.astype(BF16), v) * (1.0 / l)).astype(o_ref.dtype)
        lse_tile = jnp.where(lane == h, m + jnp.log2(l), lse_tile)
    lse_ref[0, 0] = lse_tile


def _dil_attn(qkv, *, reach, n_heads, t, name):
    batch, dil, su, _ = qkv.shape
    w = n_heads * LANES
    tp = reach
    assert t % tp == 0 and tp % 16 == 0
    cur = lambda part: (lambda b, r, i: (b, r, i, part))
    prev = lambda part: (lambda b, r, i: (b, r, jnp.maximum(i * (t // tp) - 1, 0), part))
    return pl.pallas_call(
        functools.partial(_dil_attn_kernel, reach=reach, n_heads=n_heads),
        grid=(batch, dil, su // t),
        in_specs=[pl.BlockSpec((1, 1, t, w), cur(0)),
                  pl.BlockSpec((1, 1, tp, w), prev(1)),
                  pl.BlockSpec((1, 1, t, w), cur(1)),
                  pl.BlockSpec((1, 1, tp, w), prev(2)),
                  pl.BlockSpec((1, 1, t, w), cur(2))],
        out_specs=[pl.BlockSpec((1, 1, t, w), lambda b, r, i: (b, r, i, 0)),
                   pl.BlockSpec((1, 1, t, LANES), lambda b, r, i: (b, r, i, 0))],
        out_shape=[jax.ShapeDtypeStruct((batch, dil, su, w), BF16),
                   jax.ShapeDtypeStruct((batch, dil, su, LANES), F32)],
        compiler_params=_cparams(3, 40),
        name=name,
    )(qkv, qkv, qkv, qkv, qkv)


def _mix_out_kernel(*refs, dils, n_heads):
    n_g = len(dils)
    o_refs = refs[:n_g]
    lse_refs = refs[n_g:2 * n_g]
    w_ref, r_ref, out_ref, oc_ref, uo_ref, ul_ref = refs[2 * n_g:]
    tm = oc_ref.shape[0]
    for g, d in enumerate(dils):
        for r in range(d):
            rows = pl.ds(r, tm // d, stride=d)
            for h in range(n_heads):
                uo_ref[g, h, rows, :] = o_refs[g][0, r, :, h * LANES:(h + 1) * LANES].astype(F32)
            ul_ref[g, rows, :] = lse_refs[g][0, r]
    m = ul_ref[0]
    for g in range(1, n_g):
        m = jnp.maximum(m, ul_ref[g])
    e = [jnp.exp2(ul_ref[g] - m) for g in range(n_g)]
    den = e[0]
    for g in range(1, n_g):
        den = den + e[g]
    inv = 1.0 / den
    for h in range(n_heads):
        sl = slice(h * LANES, (h + 1) * LANES)
        acc = (e[0] * inv)[:, h:h + 1] * uo_ref[0, h]
        for g in range(1, n_g):
            acc = acc + (e[g] * inv)[:, h:h + 1] * uo_ref[g, h]
        oc_ref[:, sl] = acc.astype(BF16)
    out_ref[...] = r_ref[...] + _dot(oc_ref[...], w_ref[...])


def _mix_out(os_, lses, w, r, *, seq, n_heads, tm=512, name):
    m, n = r.shape
    k = w.shape[0]
    dils = tuple(o.shape[1] for o in os_)
    nsb = seq // tm
    cls = lambda i: (i // nsb, 0, i % nsb, 0)
    return pl.pallas_call(
        functools.partial(_mix_out_kernel, dils=dils, n_heads=n_heads),
        grid=(m // tm,),
        in_specs=([pl.BlockSpec((1, d, tm // d, k), cls) for d in dils]
                  + [pl.BlockSpec((1, d, tm // d, LANES), cls) for d in dils]
                  + [pl.BlockSpec((k, n), lambda i: (0, 0)),
                     pl.BlockSpec((tm, n), lambda i: (i, 0))]),
        out_specs=pl.BlockSpec((tm, n), lambda i: (i, 0)),
        out_shape=jax.ShapeDtypeStruct((m, n), F32),
        scratch_shapes=[pltpu.VMEM((tm, k), BF16), pltpu.VMEM((len(dils), n_heads, tm, LANES), F32),
                        pltpu.VMEM((len(dils), tm, LANES), F32)],
        compiler_params=_cparams(1, 56),
        name=name,
    )(*os_, *lses, w, r)


def _cmp_kernel(z_ref, w1_ref, w2_ref, pe_ref, tab_ref, o_ref, *, n_cmp):
    w1 = w1_ref[0]
    z = z_ref[0, 0]
    nc, half = z.shape
    bias = _dot(pe_ref[0].astype(BF16), w1)[0:1]
    hid = _dot(z, w1[:half]) + pltpu.roll(_dot(z, w1[half:]), nc - 1, 0) + bias
    hid = 0.5 * hid * (1.0 + jnp.tanh(math.sqrt(2.0 / math.pi) * (hid + 0.044715 * (hid * hid * hid))))
    o = _apply_rope(_dot(hid.astype(BF16), w2_ref[0]), tab_ref[0, 0], tab_ref[0, 1])
    row = lax.broadcasted_iota(jnp.int32, o.shape, 0)
    o_ref[0, 0] = jnp.where(row < n_cmp, o, 0.0).astype(o_ref.dtype)


def _compress(z, w1, w2, pe, tab, *, n_cmp, name):
    two, bg, nc, half = z.shape
    ld = w1.shape[1]
    hid = w1.shape[2]
    return pl.pallas_call(
        functools.partial(_cmp_kernel, n_cmp=n_cmp),
        grid=(two, bg),
        in_specs=[pl.BlockSpec((1, 1, nc, half), lambda s, b: (s, b, 0, 0)),
                  pl.BlockSpec((1, ld, hid), lambda s, b: (s, 0, 0)),
                  pl.BlockSpec((1, hid, LANES), lambda s, b: (s, 0, 0)),
                  pl.BlockSpec((1, SUBLANES, ld), lambda s, b: (s, 0, 0)),
                  pl.BlockSpec((1, 2, nc, LANES), lambda s, b: (s, 0, 0, 0))],
        out_specs=pl.BlockSpec((1, 1, nc, LANES), lambda s, b: (s, b, 0, 0)),
        out_shape=jax.ShapeDtypeStruct((two, bg, nc, LANES), BF16),
        compiler_params=_cparams(2, 40),
        name=name,
    )(z, w1, w2, pe, tab)


def _nsa_kernel(q_ref, gate_ref, kc_ref, vct_ref, ks_ref, vst_ref, kw_ref, vwt_ref, ovt_ref, o_ref, kaug_ref,
                acc_ref, *, tq, n_sub, tk, hpg, top_n):
    qi = pl.program_id(2)
    r = hpg * tq
    seq = ks_ref.shape[1]
    nj = ovt_ref.shape[0]
    nc = kc_ref.shape[2]

    @pl.when(qi == 0)
    def _():
        def fill(c, carry):
            r0 = pl.multiple_of(c * tk, tk)
            kaug_ref[pl.ds(r0, tk), 0:LANES] = ks_ref[0, pl.ds(r0, tk), :]
            blk = (r0 + lax.broadcasted_iota(jnp.int32, (tk, nj), 0)) // SLC_LEN
            hot = jnp.where(blk == lax.broadcasted_iota(jnp.int32, (tk, nj), 1), 1.0, 0.0)
            kaug_ref[pl.ds(r0, tk), LANES:LANES + nj] = hot.astype(BF16)
            return carry
        lax.fori_loop(0, seq // tk, fill, 0)

    subs = range(n_sub)
    n_wt = WIN_LEN // tq + 1
    span = n_wt * tq
    t0 = [(qi * n_sub + u) * tq for u in subs]
    qs, tcol = [], []
    for u in subs:
        q = q_ref[0, u * tq:(u + 1) * tq, :]
        qs.append(jnp.concatenate([q[:, h * LANES:(h + 1) * LANES] for h in range(hpg)], axis=0))
        tcol.append(t0[u] + (lax.broadcasted_iota(jnp.int32, (1, r), 1) & (tq - 1)))

    s_c = [_nt_dot(kc_ref[0, 0], qs[u]) for u in subs]
    start = [pl.multiple_of(jnp.maximum(t0[u] - WIN_LEN, 0), tq) for u in subs]
    s_w = [_nt_dot(kw_ref[0, pl.ds(start[u], span), :], qs[u]) for u in subs]

    cend = lax.broadcasted_iota(jnp.int32, (nc, 1), 0) * CMP_STRIDE + (CMP_LEN - 1)
    ovt = ovt_ref[...]
    o_cmp, imp = [], []
    for u in subs:
        sc = jnp.where(cend <= tcol[u], s_c[u], NEG_INF)
        m_c = jnp.max(sc, axis=0, keepdims=True)
        e_c = jnp.exp2(sc - m_c)
        inv_c = jnp.where(m_c > 0.5 * NEG_INF, 1.0 / jnp.maximum(jnp.sum(e_c, axis=0, keepdims=True), TINY), 0.0)
        p_c = e_c * inv_c
        o_cmp.append(_dot(vct_ref[0, 0], p_c.astype(BF16)))
        psum = p_c[:, 0:tq]
        for h in range(1, hpg):
            psum = psum + p_c[:, h * tq:(h + 1) * tq]
        p_hi = psum.astype(BF16)
        rem = psum - p_hi.astype(F32)
        p_mid = rem.astype(BF16)
        p_lo = (rem - p_mid.astype(F32)).astype(BF16)
        imp.append(_dot(ovt, p_hi) + _dot(ovt, p_mid) + _dot(ovt, p_lo))

    acc_w = []
    for u in subs:
        dist = tcol[u] - (start[u] + lax.broadcasted_iota(jnp.int32, (span, 1), 0))
        sw = jnp.where(dist.astype(jnp.uint32) < WIN_LEN, s_w[u], NEG_INF)
        p_w = jnp.exp2(sw - jnp.max(sw, axis=0, keepdims=True)).astype(BF16)
        wt0 = start[u] // tq
        vw = jnp.concatenate([vwt_ref[0, 0, wt0 + i] for i in range(n_wt)], axis=1)
        acc_w.append(_dot(vw, p_w))

    jrow = lax.broadcasted_iota(jnp.int32, (nj, tq), 0)
    score = []
    for u in subs:
        tok = t0[u] + lax.broadcasted_iota(jnp.int32, (nj, tq), 1)
        cur = tok // SLC_LEN
        forced = (jrow == 0) | (jrow == cur) | (jrow == cur - 1)
        score.append(jnp.where(forced, FORCE_SCORE, jnp.where(jrow * SLC_LEN <= tok, imp[u], NEG_INF)))
    for _ in range(top_n):
        for u in subs:
            mx = jnp.max(score[u], axis=0, keepdims=True)
            first = jnp.min(jnp.where(score[u] == mx, jrow, nj), axis=0, keepdims=True)
            score[u] = jnp.where(jrow == first, REMOVED_SCORE, score[u])
    q_aug = []
    for u in subs:
        bias = jnp.where(score[u] < 0.5 * REMOVED_SCORE, 0.0, NEG_INF).T.astype(BF16)
        q_aug.append(jnp.concatenate([qs[u], jnp.concatenate([bias] * hpg, axis=0)], axis=1))

    def slc_tile(kj, m_run, diagonal):
        k0 = pl.multiple_of(kj * tk, tk)
        ka = kaug_ref[pl.ds(k0, tk), :]
        vt = vst_ref[0, 0, kj]
        out = []
        scores = [_nt_dot(ka, q_aug[u]) for u in subs]
        for u in subs:
            s = scores[u]
            if diagonal:
                s = jnp.where(k0 + lax.broadcasted_iota(jnp.int32, (tk, 1), 0) <= tcol[u], s, NEG_INF)
            m_new = jnp.maximum(m_run[u], jnp.max(s, axis=0, keepdims=True))
            p = jnp.exp2(s - m_new).astype(BF16)
            acc_ref[u] = jnp.exp2(m_run[u] - m_new) * acc_ref[u] + _dot(vt, p)
            out.append(m_new)
        return tuple(out)

    n_full = (qi * n_sub * tq) // tk
    for u in subs:
        acc_ref[u] = jnp.zeros((LANES + SUBLANES, r), F32)
    m_run = (jnp.full((1, r), NEG_INF, F32),) * n_sub
    m_run = lax.fori_loop(0, n_full, lambda kj, c: slc_tile(kj, c, False), m_run)
    slc_tile(n_full, m_run, True)

    for u in subs:
        acc_s = acc_ref[u]
        g_t = gate_ref[0, u * tq:(u + 1) * tq, :].T

        def gate_row(br):
            return jnp.concatenate([g_t[br * hpg + h:br * hpg + h + 1, :] for h in range(hpg)], axis=1)

        o_t = (gate_row(0) * o_cmp[u]
               + (gate_row(1) * (1.0 / acc_s[LANES:LANES + 1])) * acc_s[0:LANES]
               + (gate_row(2) * (1.0 / acc_w[u][LANES:LANES + 1])) * acc_w[u][0:LANES])
        for h in range(hpg):
            o_ref[0, u * tq:(u + 1) * tq, h * LANES:(h + 1) * LANES] = (
                o_t[:, h * tq:(h + 1) * tq].T.astype(o_ref.dtype))


def _nsa(q, gates, kv, kc, vct, vst, vwt, overlap_t, *, n_groups, hpg, top_n, tq, n_sub, tk, name):
    b, s, _ = q.shape
    nc = kc.shape[2]
    nj = overlap_t.shape[0]
    g_ = n_groups
    seg = lambda sidx: (lambda bi, gi, qi: (bi, 0, sidx * g_ + gi))
    bg = lambda bi, gi, qi: (bi * g_ + gi, 0, 0, 0)
    vrows = LANES + SUBLANES
    tqs = n_sub * tq
    assert tk % tqs == 0 and s % tk == 0
    return pl.pallas_call(
        functools.partial(_nsa_kernel, tq=tq, n_sub=n_sub, tk=tk, hpg=hpg, top_n=top_n),
        grid=(b, g_, s // tqs),
        in_specs=[pl.BlockSpec((1, tqs, hpg * LANES), lambda bi, gi, qi: (bi, qi, gi)),
                  pl.BlockSpec((1, tqs, LANES), lambda bi, gi, qi: (bi, qi, gi)),
                  pl.BlockSpec((1, 1, nc, LANES), bg),
                  pl.BlockSpec((1, 1, LANES, nc), bg),
                  pl.BlockSpec((1, s, LANES), seg(KV_K_SLC)),
                  pl.BlockSpec((1, 1, s // tk, vrows, tk), lambda bi, gi, qi: (bi, gi, 0, 0, 0)),
                  pl.BlockSpec((1, s, LANES), seg(KV_K_WIN)),
                  pl.BlockSpec((1, 1, s // tq, vrows, tq), lambda bi, gi, qi: (bi, gi, 0, 0, 0)),
                  pl.BlockSpec((nj, nc), lambda bi, gi, qi: (0, 0))],
        out_specs=pl.BlockSpec((1, tqs, hpg * LANES), lambda bi, gi, qi: (bi, qi, gi)),
        out_shape=jax.ShapeDtypeStruct(q.shape, BF16),
        scratch_shapes=[pltpu.VMEM((s, LANES + nj), BF16),
                        pltpu.VMEM((n_sub, vrows, hpg * tq), F32)],
        compiler_params=_cparams(3, 48),
        name=name,
    )(q, gates, kc, vct, kv, vst, kv, vwt, overlap_t)


KV_SEGS = (2, 4, 0, 1, 3, 5)
KV_ROPE_SRC = (2, 4)
KV_K_SLC, KV_K_WIN, KV_K_CMP, KV_V_CMP, KV_V_SLC, KV_V_WIN = range(6)

NSA_TQ = 128
NSA_SUB = 4
NSA_TK = 512


def _dilated_layer(h, g_attn, w_in, w_out, tables, batch, seq):
    m, d = h.shape
    ng = len(DILATED_GROUPS)
    dils = tuple(dl for _, dl in DILATED_GROUPS)
    hg = w_out.shape[0] // HEAD_DIM
    wd = hg * HEAD_DIM
    reach = DILATED_GROUPS[0][0] // DILATED_GROUPS[0][1]
    assert all(win // dl == reach for win, dl in DILATED_GROUPS)
    w4 = w_in.reshape(d, 3, ng, wd)
    ws = [jnp.concatenate([_perm_head_cols(w4[:, 0, gi]), _perm_head_cols(w4[:, 1, gi]), w4[:, 2, gi]],
                          axis=1).astype(BF16) for gi in range(ng)]
    tm = min(512, seq)
    os_, lses = [], []
    for gi, dil in enumerate(dils):
        qkv = _a_qkv(h, g_attn, ws[gi], tables, batch=batch, seq=seq, dil=dil,
                     part_width=wd, tm=tm, name=f"a_qkv{gi}")
        o, lse = _dil_attn(qkv, reach=reach, n_heads=hg, t=min(256, seq // dil), name=f"a_attn{gi}")
        os_.append(o)
        lses.append(lse)
    return _mix_out(os_, lses, w_out.astype(BF16), h, seq=seq, n_heads=hg, tm=tm, name="a_out")


def _value_tiles_t(v, batch, seq, g_, tile):
    vt = v.reshape(batch, seq // tile, tile, g_, HEAD_DIM).transpose(0, 3, 1, 4, 2)
    ones = jnp.ones((batch, g_, seq // tile, SUBLANES, tile), BF16)
    return jnp.concatenate([vt, ones], axis=3)


def _nsa_shared_kv(h, kv_norm_g, w_kv, cmp_k, cmp_v, tables, batch, seq):
    g_ = NSA_KV_GROUPS
    seg_w = g_ * HEAD_DIM
    (pe_k, w1_k, w2_k), (pe_v, w1_v, w2_v) = cmp_k, cmp_v
    src = lambda s: w_kv[:, s * seg_w:(s + 1) * seg_w]
    w = jnp.concatenate([_perm_head_cols(src(s)) if s in KV_ROPE_SRC else src(s) for s in KV_SEGS],
                        axis=1).astype(BF16)
    kv = _norm_matmul(h, kv_norm_g, w, out_dtype=BF16, tables=tables, seq=seq,
                      tab_fn=lambda j: jnp.where(j == 0, ROPE_K, ROPE_NONE), tn=len(KV_ROPE_SRC) * seg_w,
                      name="b_kv")
    n_chunk = seq // CMP_STRIDE
    n_cmp = (seq - CMP_LEN) // CMP_STRIDE + 1
    assert CMP_LEN == 2 * CMP_STRIDE

    assert KV_V_CMP == KV_K_CMP + 1
    z = kv[:, KV_K_CMP * seg_w:(KV_V_CMP + 1) * seg_w].reshape(batch, n_chunk, CMP_STRIDE, 2, g_, HEAD_DIM)
    z = z.transpose(3, 0, 4, 1, 2, 5).reshape(2, batch * g_, n_chunk, CMP_STRIDE * HEAD_DIM)
    pe = jnp.stack([pe_k.reshape(1, -1), pe_v.reshape(1, -1)])
    pe = jnp.broadcast_to(pe, (2, SUBLANES, pe.shape[-1]))
    cmp_tab = _rope_tables(jnp.arange(n_chunk) * CMP_STRIDE + CMP_LEN - 1, (1.0,))
    cmp_kv = _compress(z, jnp.stack([w1_k, w1_v]).astype(BF16),
                       jnp.stack([_perm_head_cols(w2_k), w2_v]).astype(BF16),
                       pe, cmp_tab, n_cmp=n_cmp, name="b_compress")
    kc = cmp_kv[0][:, None]
    vct = cmp_kv[1].transpose(0, 2, 1)[:, None]
    vst = _value_tiles_t(kv[:, KV_V_SLC * seg_w:(KV_V_SLC + 1) * seg_w], batch, seq, g_, min(NSA_TK, seq))
    vwt = _value_tiles_t(kv[:, KV_V_WIN * seg_w:(KV_V_WIN + 1) * seg_w], batch, seq, g_, NSA_TQ)
    return kv, kc, vct, vst, vwt


def _nsa_layer(h, g_attn, w_qg, w_out, shared, tables, batch, seq):
    m, d = h.shape
    g_ = NSA_KV_GROUPS
    n_heads = w_out.shape[0] // HEAD_DIM
    hpg = n_heads // g_
    seg_w = g_ * HEAD_DIM
    kv, kc, vct, vst, vwt = shared
    n_chunk = seq // CMP_STRIDE
    n_cmp = (seq - CMP_LEN) // CMP_STRIDE + 1
    n_q = n_heads * HEAD_DIM
    q = _norm_matmul(h, g_attn, _perm_head_cols(w_qg[:, :n_q]).astype(BF16), out_dtype=BF16,
                     tables=tables, seq=seq, tab_fn=lambda j: ROPE_Q, tn=1024,
                     name="b_q")
    wg = w_qg[:, n_q:].reshape(d, N_NSA_BRANCH, g_, hpg).transpose(0, 2, 1, 3).reshape(d, g_, N_NSA_BRANCH * hpg)
    wg = jnp.pad(wg, ((0, 0), (0, 0), (0, LANES - N_NSA_BRANCH * hpg))).reshape(d, g_ * LANES)
    gates = _norm_matmul(h, g_attn, wg.astype(BF16), out_dtype=F32, act="sigmoid", tn=g_ * LANES,
                         name="b_gates")

    n_slc = seq // SLC_LEN
    nj = -(-n_slc // LANES) * LANES
    cs = np.arange(n_chunk)[:, None] * CMP_STRIDE
    ss = np.arange(nj)[None, :] * SLC_LEN
    ov = np.clip(np.minimum(cs + CMP_LEN, ss + SLC_LEN) - np.maximum(cs, ss), 0, None) / CMP_LEN
    ov[n_cmp:, :] = 0.0
    overlap_t = jnp.asarray(ov.T, BF16)

    o = _nsa(q.reshape(batch, seq, n_q), gates.reshape(batch, seq, g_ * LANES),
             kv.reshape(batch, seq, 6 * seg_w), kc, vct, vst, vwt, overlap_t,
             n_groups=g_, hpg=hpg, top_n=min(SLC_TOP_N, n_slc), tq=NSA_TQ, n_sub=NSA_SUB, tk=min(NSA_TK, seq),
             name="b_nsa")
    return _matmul_resid(o.reshape(m, n_q), w_out.astype(BF16), h, name="b_out")


def kernel(x, p, a_w_in, a_w_out, b_w_qg, b_w_out, kv_norm_g, w_kv_shared, cmp_pe_k, cmp_w1_k, cmp_w2_k,
           cmp_pe_v, cmp_w1_v, cmp_w2_v, attn_norm_g, mlp_norm_g, mlp_w1, mlp_w2, ple_norm_g, ple_w_gate,
           ple_w_proj, final_norm_g):
    batch, seq, d = x.shape
    m = batch * seq
    depth = attn_norm_g.shape[0]
    n_a = a_w_in.shape[0]
    h = x.reshape(m, d)
    tables = _seq_rope_tables(seq)
    shared = None
    for i in range(depth):
        if i < n_a:
            h = _dilated_layer(h, attn_norm_g[i], a_w_in[i], a_w_out[i], tables, batch, seq)
        else:
            if i == n_a:
                shared = _nsa_shared_kv(h, kv_norm_g, w_kv_shared, (cmp_pe_k, cmp_w1_k, cmp_w2_k),
                                        (cmp_pe_v, cmp_w1_v, cmp_w2_v), tables, batch, seq)
            j = i - n_a
            h = _nsa_layer(h, attn_norm_g[i], b_w_qg[j], b_w_out[j], shared, tables, batch, seq)
        h = _mlp(h, mlp_norm_g[i], mlp_w1[i].astype(BF16), mlp_w2[i].astype(BF16), name=f"mlp{i}")
        h = _ple(h, ple_norm_g[i], ple_w_gate[i].astype(BF16), p[i].reshape(m, -1),
                 ple_w_proj[i].astype(BF16), final_g=final_norm_g if i == depth - 1 else None, name=f"ple{i}")
    if depth == 0:
        h = _rmsnorm(h, final_norm_g, name="final_norm")
    return h.reshape(batch, seq, d)
```

```python
import functools
import math

import jax
import jax.numpy as jnp
import numpy as np
from jax import lax
from jax.experimental import pallas as pl
from jax.experimental.pallas import tpu as pltpu

F32 = jnp.float32
BF16 = jnp.bfloat16

HEAD_DIM = 128
ROT_DIM = HEAD_DIM // 4
ROPE_THETA = 500000.0
NORM_EPS = 1e-6
NEG_INF = -1e30
FORCE_SCORE = 1e9
TINY = 1e-20
REMOVED_SCORE = -3.0e38
LOG2E = math.log2(math.e)

DILATED_GROUPS = ((128, 1), (512, 4), (2048, 16))
NSA_KV_GROUPS = 4
CMP_LEN = 32
CMP_STRIDE = 16
SLC_LEN = 64
SLC_TOP_N = 16
WIN_LEN = 512
N_NSA_BRANCH = 3

LANES = 128
SUBLANES = 8
MIB = 1024 * 1024


def _cparams(n_grid, vmem_mib):
    return pltpu.CompilerParams(dimension_semantics=("arbitrary",) * n_grid,
                                vmem_limit_bytes=vmem_mib * MIB)


def _nt_dot(a, b):
    return lax.dot_general(a, b, (((1,), (1,)), ((), ())), preferred_element_type=F32)


def _dot(a, b):
    return jnp.dot(a, b, preferred_element_type=F32)


def _perm_head_cols(w):
    k, n = w.shape
    half = ROT_DIM // 2
    w = w.reshape(k, n // HEAD_DIM, HEAD_DIM)
    w = jnp.concatenate([w[..., :half], w[..., ROT_DIM:LANES // 2 + half], w[..., half:ROT_DIM],
                         w[..., LANES // 2 + half:]], axis=-1)
    return w.reshape(k, n)


def _rope_tables(pos, scales):
    n = pos.shape[0]
    half = ROT_DIM // 2
    inv = ROPE_THETA ** (-jnp.arange(half, dtype=F32) * (2.0 / ROT_DIM))
    ang = pos.astype(F32)[:, None] * inv[None, :]
    c, s = jnp.cos(ang), jnp.sin(ang)
    ones = jnp.ones((n, LANES // 2 - half), F32)
    zeros = jnp.zeros((n, LANES // 2 - half), F32)
    rot = jnp.stack([jnp.concatenate([c, ones, c, ones], axis=1),
                     jnp.concatenate([-s, zeros, s, zeros], axis=1)])
    ident = jnp.stack([jnp.ones((n, LANES), F32), jnp.zeros((n, LANES), F32)])
    return jnp.stack([rot * sc for sc in scales] + [ident])


ROPE_Q, ROPE_K, ROPE_NONE = range(3)


def _seq_rope_tables(seq):
    return _rope_tables(jnp.arange(seq), (HEAD_DIM ** -0.5 * LOG2E, 1.0))


def _apply_rope(a, cos, sg):
    return a * cos + pltpu.roll(a, LANES // 2, 1) * sg


def _rms_scale(x, g):
    ms = jnp.mean(x * x, axis=-1, keepdims=True)
    return (x * lax.rsqrt(ms + NORM_EPS) * g).astype(BF16)


def _norm_matmul_kernel(x_ref, g_ref, w_ref, *rest, rope, act):
    if rope:
        tab_ref, o_ref, xn_ref = rest
    else:
        o_ref, xn_ref = rest

    @pl.when(pl.program_id(1) == 0)
    def _():
        xn_ref[...] = _rms_scale(x_ref[...], g_ref[...])

    acc = _dot(xn_ref[...], w_ref[...])
    if rope:
        cos, sg = tab_ref[0, 0], tab_ref[0, 1]
        for c in range(acc.shape[1] // LANES):
            sl = slice(c * LANES, (c + 1) * LANES)
            o_ref[:, sl] = _apply_rope(acc[:, sl], cos, sg).astype(o_ref.dtype)
    else:
        if act == "sigmoid":
            acc = 1.0 / (1.0 + jnp.exp(-acc))
        o_ref[...] = acc.astype(o_ref.dtype)


def _norm_matmul(x, g, w, *, out_dtype, tables=None, tab_fn=None, seq=None, act=None,
                 tm=512, tn=512, name):
    m, k = x.shape
    n = w.shape[1]
    tm, tn = min(tm, m), min(tn, n)
    in_specs = [pl.BlockSpec((tm, k), lambda i, j: (i, 0)),
                pl.BlockSpec((1, k), lambda i, j: (0, 0)),
                pl.BlockSpec((k, tn), lambda i, j: (0, j))]
    args = [x, g.reshape(1, k), w]
    if tables is not None:
        nsb = seq // tm
        in_specs.append(pl.BlockSpec((1, 2, tm, LANES), lambda i, j: (tab_fn(j), 0, i % nsb, 0)))
        args.append(tables)
    return pl.pallas_call(
        functools.partial(_norm_matmul_kernel, rope=tables is not None, act=act),
        grid=(m // tm, n // tn),
        in_specs=in_specs,
        out_specs=pl.BlockSpec((tm, tn), lambda i, j: (i, j)),
        out_shape=jax.ShapeDtypeStruct((m, n), out_dtype),
        scratch_shapes=[pltpu.VMEM((tm, k), BF16)],
        compiler_params=_cparams(2, 40),
        name=name,
    )(*args)


def _matmul_resid_kernel(a_ref, w_ref, r_ref, o_ref):
    o_ref[...] = r_ref[...] + _dot(a_ref[...], w_ref[...])


def _matmul_resid(a, w, r, *, tm=512, tn=1024, name):
    m, k = a.shape
    n = w.shape[1]
    tm, tn = min(tm, m), min(tn, n)
    return pl.pallas_call(
        _matmul_resid_kernel,
        grid=(m // tm, n // tn),
        in_specs=[pl.BlockSpec((tm, k), lambda i, j: (i, 0)),
                  pl.BlockSpec((k, tn), lambda i, j: (0, j)),
                  pl.BlockSpec((tm, tn), lambda i, j: (i, j))],
        out_specs=pl.BlockSpec((tm, tn), lambda i, j: (i, j)),
        out_shape=jax.ShapeDtypeStruct((m, n), F32),
        compiler_params=_cparams(2, 40),
        name=name,
    )(a, w, r)


def _mlp_kernel(x_ref, g_ref, w1_ref, w2_ref, o_ref, xn_ref, acc_ref):
    f = pl.program_id(1)

    @pl.when(f == 0)
    def _():
        xn_ref[...] = _rms_scale(x_ref[...], g_ref[...])
        acc_ref[...] = jnp.zeros_like(acc_ref)

    a = _dot(xn_ref[...], w1_ref[...])
    a = jnp.square(jnp.maximum(a, 0.0)).astype(BF16)
    acc_ref[...] += _dot(a, w2_ref[...])

    @pl.when(f == pl.num_programs(1) - 1)
    def _():
        o_ref[...] = x_ref[...] + acc_ref[...]


def _mlp(h, g, w1, w2, *, tm=512, tf=1024, name):
    m, d = h.shape
    ff = w1.shape[1]
    tm, tf = min(tm, m), min(tf, ff)
    return pl.pallas_call(
        _mlp_kernel,
        grid=(m // tm, ff // tf),
        in_specs=[pl.BlockSpec((tm, d), lambda i, f: (i, 0)),
                  pl.BlockSpec((1, d), lambda i, f: (0, 0)),
                  pl.BlockSpec((d, tf), lambda i, f: (0, f)),
                  pl.BlockSpec((tf, d), lambda i, f: (f, 0))],
        out_specs=pl.BlockSpec((tm, d), lambda i, f: (i, 0)),
        out_shape=jax.ShapeDtypeStruct((m, d), F32),
        scratch_shapes=[pltpu.VMEM((tm, d), BF16), pltpu.VMEM((tm, d), F32)],
        compiler_params=_cparams(2, 56),
        name=name,
    )(h, g.reshape(1, d), w1, w2)


PLE_COL_CHUNK = 1024


def _ple_kernel(x_ref, g_ref, wg_ref, p_ref, wp_ref, *rest, final):
    if final:
        fg_ref, o_ref = rest
    else:
        (o_ref,) = rest
    n = o_ref.shape[1]
    xn = _rms_scale(x_ref[...], g_ref[...])
    pb = p_ref[...].astype(BF16)
    ssq = None
    for c0 in range(0, n, PLE_COL_CHUNK):
        sl = slice(c0, c0 + PLE_COL_CHUNK)
        gate = 1.0 / (1.0 + jnp.exp(-_dot(xn, wg_ref[:, sl])))
        hc = x_ref[:, sl] + _dot(pb, wp_ref[:, sl]) * gate
        o_ref[:, sl] = hc
        if final:
            part = jnp.sum(hc * hc, axis=-1, keepdims=True)
            ssq = part if ssq is None else ssq + part
    if final:
        o_ref[...] = o_ref[...] * lax.rsqrt(ssq * (1.0 / n) + NORM_EPS) * fg_ref[...]


def _ple(h, g, wg, p, wp, *, final_g=None, tm=512, name):
    m, d = h.shape
    dp = p.shape[1]
    n = wg.shape[1]
    assert n == d and n % PLE_COL_CHUNK == 0
    tm = min(tm, m)
    row = lambda i: (i, 0)
    const = lambda i: (0, 0)
    in_specs = [pl.BlockSpec((tm, d), row), pl.BlockSpec((1, d), const), pl.BlockSpec((d, n), const),
                pl.BlockSpec((tm, dp), row), pl.BlockSpec((dp, n), const)]
    args = [h, g.reshape(1, d), wg, p, wp]
    if final_g is not None:
        in_specs.append(pl.BlockSpec((1, n), const))
        args.append(final_g.reshape(1, n))
    return pl.pallas_call(
        functools.partial(_ple_kernel, final=final_g is not None),
        grid=(m // tm,),
        in_specs=in_specs,
        out_specs=pl.BlockSpec((tm, n), row),
        out_shape=jax.ShapeDtypeStruct((m, n), F32),
        compiler_params=_cparams(1, 56),
        name=name,
    )(*args)


def _rmsnorm_kernel(x_ref, g_ref, o_ref):
    x = x_ref[...]
    ms = jnp.mean(x * x, axis=-1, keepdims=True)
    o_ref[...] = x * lax.rsqrt(ms + NORM_EPS) * g_ref[...]


def _rmsnorm(h, g, *, tm=512, name):
    m, d = h.shape
    tm = min(tm, m)
    return pl.pallas_call(
        _rmsnorm_kernel,
        grid=(m // tm,),
        in_specs=[pl.BlockSpec((tm, d), lambda i: (i, 0)),
                  pl.BlockSpec((1, d), lambda i: (0, 0))],
        out_specs=pl.BlockSpec((tm, d), lambda i: (i, 0)),
        out_shape=jax.ShapeDtypeStruct((m, d), F32),
        compiler_params=_cparams(1, 32),
        name=name,
    )(h, g.reshape(1, d))


def _a_qkv_kernel(x_ref, g_ref, w_ref, tab_ref, o_ref, xn_ref, res_ref, *, dil):
    @pl.when(pl.program_id(1) == 0)
    def _():
        xn_ref[...] = _rms_scale(x_ref[...], g_ref[...])

    acc = _dot(xn_ref[...], w_ref[...])
    tm, tn = acc.shape
    cos, sg = tab_ref[0, 0], tab_ref[0, 1]
    for c in range(tn // LANES):
        sl = slice(c * LANES, (c + 1) * LANES)
        res = _apply_rope(acc[:, sl], cos, sg)
        if dil == 1:
            o_ref[0, 0, :, sl] = res.astype(BF16)
        else:
            res_ref[c] = res
            for r in range(dil):
                o_ref[0, r, :, sl] = res_ref[c, pl.ds(r, tm // dil, stride=dil), :].astype(BF16)


def _a_qkv(x, g, w, tables, *, batch, seq, dil, part_width, tm=512, tn=1024, name):
    m, k = x.shape
    n = w.shape[1]
    nsb = seq // tm
    bpp = part_width // tn
    assert tm % (dil * 16) == 0
    return pl.pallas_call(
        functools.partial(_a_qkv_kernel, dil=dil),
        grid=(m // tm, n // tn),
        in_specs=[pl.BlockSpec((tm, k), lambda i, j: (i, 0)),
                  pl.BlockSpec((1, k), lambda i, j: (0, 0)),
                  pl.BlockSpec((k, tn), lambda i, j: (0, j)),
                  pl.BlockSpec((1, 2, tm, LANES), lambda i, j: (j // bpp, 0, i % nsb, 0))],
        out_specs=pl.BlockSpec((1, dil, tm // dil, tn), lambda i, j: (i // nsb, 0, i % nsb, j)),
        out_shape=jax.ShapeDtypeStruct((batch, dil, seq // dil, n), BF16),
        scratch_shapes=[pltpu.VMEM((tm, k), BF16), pltpu.VMEM((tn // LANES, tm, LANES), F32)],
        compiler_params=_cparams(2, 40),
        name=name,
    )(x, g.reshape(1, k), w, tables)


def _dil_attn_kernel(q_ref, kp_ref, kc_ref, vp_ref, vc_ref, o_ref, lse_ref, *, reach, n_heads):
    i = pl.program_id(2)
    t = q_ref.shape[2]
    tp = kp_ref.shape[2]
    row = lax.broadcasted_iota(jnp.int32, (t, tp + t), 0)
    col = lax.broadcasted_iota(jnp.int32, (t, tp + t), 1)
    no_prev = jnp.where(i > 0, 0, reach + 1 + tp + t)
    dist = row + tp - col + jnp.where(col < tp, no_prev, 0)
    valid = dist.astype(jnp.uint32) <= reach
    lane = lax.broadcasted_iota(jnp.int32, (t, LANES), 1)
    lse_tile = jnp.zeros((t, LANES), F32)
    heads = [slice(h * LANES, (h + 1) * LANES) for h in range(n_heads)]
    scores = [_nt_dot(q_ref[0, 0, :, sl], jnp.concatenate([kp_ref[0, 0, :, sl], kc_ref[0, 0, :, sl]], axis=0))
              for sl in heads]
    for h, sl in enumerate(heads):
        v = jnp.concatenate([vp_ref[0, 0, :, sl], vc_ref[0, 0, :, sl]], axis=0)
        s = jnp.where(valid, scores[h], NEG_INF)
        m = jnp.max(s, axis=-1, keepdims=True)
        p = jnp.exp2(s - m)
        l = jnp.sum(p, axis=-1, keepdims=True)
        o_ref[0, 0, :, sl] = (_dot(p.astype(BF16), v) * (1.0 / l)).astype(o_ref.dtype)
        lse_tile = jnp.where(lane == h, m + jnp.log2(l), lse_tile)
    lse_ref[0, 0] = lse_tile


def _dil_attn(qkv, *, reach, n_heads, t, name):
    batch, dil, su, _ = qkv.shape
    w = n_heads * LANES
    tp = reach
    assert t % tp == 0 and tp % 16 == 0
    cur = lambda part: (lambda b, r, i: (b, r, i, part))
    prev = lambda part: (lambda b, r, i: (b, r, jnp.maximum(i * (t // tp) - 1, 0), part))
    return pl.pallas_call(
        functools.partial(_dil_attn_kernel, reach=reach, n_heads=n_heads),
        grid=(batch, dil, su // t),
        in_specs=[pl.BlockSpec((1, 1, t, w), cur(0)),
                  pl.BlockSpec((1, 1, tp, w), prev(1)),
                  pl.BlockSpec((1, 1, t, w), cur(1)),
                  pl.BlockSpec((1, 1, tp, w), prev(2)),
                  pl.BlockSpec((1, 1, t, w), cur(2))],
        out_specs=[pl.BlockSpec((1, 1, t, w), lambda b, r, i: (b, r, i, 0)),
                   pl.BlockSpec((1, 1, t, LANES), lambda b, r, i: (b, r, i, 0))],
        out_shape=[jax.ShapeDtypeStruct((batch, dil, su, w), BF16),
                   jax.ShapeDtypeStruct((batch, dil, su, LANES), F32)],
        compiler_params=_cparams(3, 40),
        name=name,
    )(qkv, qkv, qkv, qkv, qkv)


def _mix_out_kernel(*refs, dils, n_heads):
    n_g = len(dils)
    o_refs = refs[:n_g]
    lse_refs = refs[n_g:2 * n_g]
    w_ref, r_ref, out_ref, oc_ref, uo_ref, ul_ref = refs[2 * n_g:]
    tm = oc_ref.shape[0]
    for g, d in enumerate(dils):
        for r in range(d):
            rows = pl.ds(r, tm // d, stride=d)
            for h in range(n_heads):
                uo_ref[g, h, rows, :] = o_refs[g][0, r, :, h * LANES:(h + 1) * LANES].astype(F32)
            ul_ref[g, rows, :] = lse_refs[g][0, r]
    m = ul_ref[0]
    for g in range(1, n_g):
        m = jnp.maximum(m, ul_ref[g])
    e = [jnp.exp2(ul_ref[g] - m) for g in range(n_g)]
    den = e[0]
    for g in range(1, n_g):
        den = den + e[g]
    inv = 1.0 / den
    for h in range(n_heads):
        sl = slice(h * LANES, (h + 1) * LANES)
        acc = (e[0] * inv)[:, h:h + 1] * uo_ref[0, h]
        for g in range(1, n_g):
            acc = acc + (e[g] * inv)[:, h:h + 1] * uo_ref[g, h]
        oc_ref[:, sl] = acc.astype(BF16)
    out_ref[...] = r_ref[...] + _dot(oc_ref[...], w_ref[...])


def _mix_out(os_, lses, w, r, *, seq, n_heads, tm=512, name):
    m, n = r.shape
    k = w.shape[0]
    dils = tuple(o.shape[1] for o in os_)
    nsb = seq // tm
    cls = lambda i: (i // nsb, 0, i % nsb, 0)
    return pl.pallas_call(
        functools.partial(_mix_out_kernel, dils=dils, n_heads=n_heads),
        grid=(m // tm,),
        in_specs=([pl.BlockSpec((1, d, tm // d, k), cls) for d in dils]
                  + [pl.BlockSpec((1, d, tm // d, LANES), cls) for d in dils]
                  + [pl.BlockSpec((k, n), lambda i: (0, 0)),
                     pl.BlockSpec((tm, n), lambda i: (i, 0))]),
        out_specs=pl.BlockSpec((tm, n), lambda i: (i, 0)),
        out_shape=jax.ShapeDtypeStruct((m, n), F32),
        scratch_shapes=[pltpu.VMEM((tm, k), BF16), pltpu.VMEM((len(dils), n_heads, tm, LANES), F32),
                        pltpu.VMEM((len(dils), tm, LANES), F32)],
        compiler_params=_cparams(1, 56),
        name=name,
    )(*os_, *lses, w, r)


def _cmp_kernel(z_ref, w1_ref, w2_ref, pe_ref, tab_ref, o_ref, *, n_cmp):
    w1 = w1_ref[0]
    z = z_ref[0, 0]
    nc, half = z.shape
    bias = _dot(pe_ref[0].astype(BF16), w1)[0:1]
    hid = _dot(z, w1[:half]) + pltpu.roll(_dot(z, w1[half:]), nc - 1, 0) + bias
    hid = 0.5 * hid * (1.0 + jnp.tanh(math.sqrt(2.0 / math.pi) * (hid + 0.044715 * (hid * hid * hid))))
    o = _apply_rope(_dot(hid.astype(BF16), w2_ref[0]), tab_ref[0, 0], tab_ref[0, 1])
    row = lax.broadcasted_iota(jnp.int32, o.shape, 0)
    o_ref[0, 0] = jnp.where(row < n_cmp, o, 0.0).astype(o_ref.dtype)


def _compress(z, w1, w2, pe, tab, *, n_cmp, name):
    two, bg, nc, half = z.shape
    ld = w1.shape[1]
    hid = w1.shape[2]
    return pl.pallas_call(
        functools.partial(_cmp_kernel, n_cmp=n_cmp),
        grid=(two, bg),
        in_specs=[pl.BlockSpec((1, 1, nc, half), lambda s, b: (s, b, 0, 0)),
                  pl.BlockSpec((1, ld, hid), lambda s, b: (s, 0, 0)),
                  pl.BlockSpec((1, hid, LANES), lambda s, b: (s, 0, 0)),
                  pl.BlockSpec((1, SUBLANES, ld), lambda s, b: (s, 0, 0)),
                  pl.BlockSpec((1, 2, nc, LANES), lambda s, b: (s, 0, 0, 0))],
        out_specs=pl.BlockSpec((1, 1, nc, LANES), lambda s, b: (s, b, 0, 0)),
        out_shape=jax.ShapeDtypeStruct((two, bg, nc, LANES), BF16),
        compiler_params=_cparams(2, 40),
        name=name,
    )(z, w1, w2, pe, tab)


def _nsa_kernel(q_ref, gate_ref, kc_ref, vct_ref, ks_ref, vs_ref, kw_ref, vw_ref, ovt_ref, o_ref, kaug_ref,
                vst_ref, vwt_ref, acc_ref, *, tq, n_sub, tk, hpg, top_n):
    qi = pl.program_id(2)
    r = hpg * tq
    seq = ks_ref.shape[1]
    nj = ovt_ref.shape[0]
    nc = kc_ref.shape[2]

    @pl.when(qi == 0)
    def _():
        def fill(c, carry):
            r0 = pl.multiple_of(c * tk, tk)
            kaug_ref[pl.ds(r0, tk), 0:LANES] = ks_ref[0, pl.ds(r0, tk), :]
            blk = (r0 + lax.broadcasted_iota(jnp.int32, (tk, nj), 0)) // SLC_LEN
            hot = jnp.where(blk == lax.broadcasted_iota(jnp.int32, (tk, nj), 1), 1.0, 0.0)
            kaug_ref[pl.ds(r0, tk), LANES:LANES + nj] = hot.astype(BF16)
            per = tk // tq
            for src_ref, dst_ref, width, base in ((vs_ref, vst_ref, tk, c), (vw_ref, vwt_ref, tq, c * per)):
                vt = src_ref[0, pl.ds(r0, tk), :].astype(F32).T.astype(BF16)
                for i in range(tk // width):
                    dst_ref[base + i, 0:LANES, :] = vt[:, i * width:(i + 1) * width]
                    dst_ref[base + i, LANES:LANES + SUBLANES, :] = jnp.ones((SUBLANES, width), BF16)
            return carry
        lax.fori_loop(0, seq // tk, fill, 0)

    subs = range(n_sub)
    n_wt = WIN_LEN // tq + 1
    span = n_wt * tq
    t0 = [(qi * n_sub + u) * tq for u in subs]
    qs, tcol = [], []
    for u in subs:
        q = q_ref[0, u * tq:(u + 1) * tq, :]
        qs.append(jnp.concatenate([q[:, h * LANES:(h + 1) * LANES] for h in range(hpg)], axis=0))
        tcol.append(t0[u] + (lax.broadcasted_iota(jnp.int32, (1, r), 1) & (tq - 1)))

    s_c = [_nt_dot(kc_ref[0, 0], qs[u]) for u in subs]
    start = [pl.multiple_of(jnp.maximum(t0[u] - WIN_LEN, 0), tq) for u in subs]
    s_w = [_nt_dot(kw_ref[0, pl.ds(start[u], span), :], qs[u]) for u in subs]

    cend = lax.broadcasted_iota(jnp.int32, (nc, 1), 0) * CMP_STRIDE + (CMP_LEN - 1)
    ovt = ovt_ref[...]
    o_cmp, imp = [], []
    for u in subs:
        sc = jnp.where(cend <= tcol[u], s_c[u], NEG_INF)
        m_c = jnp.max(sc, axis=0, keepdims=True)
        e_c = jnp.exp2(sc - m_c)
        inv_c = jnp.where(m_c > 0.5 * NEG_INF, 1.0 / jnp.maximum(jnp.sum(e_c, axis=0, keepdims=True), TINY), 0.0)
        p_c = e_c * inv_c
        o_cmp.append(_dot(vct_ref[0, 0], p_c.astype(BF16)))
        psum = p_c[:, 0:tq]
        for h in range(1, hpg):
            psum = psum + p_c[:, h * tq:(h + 1) * tq]
        p_hi = psum.astype(BF16)
        rem = psum - p_hi.astype(F32)
        p_mid = rem.astype(BF16)
        p_lo = (rem - p_mid.astype(F32)).astype(BF16)
        imp.append(_dot(ovt, p_hi) + _dot(ovt, p_mid) + _dot(ovt, p_lo))

    acc_w = []
    for u in subs:
        dist = tcol[u] - (start[u] + lax.broadcasted_iota(jnp.int32, (span, 1), 0))
        sw = jnp.where(dist.astype(jnp.uint32) < WIN_LEN, s_w[u], NEG_INF)
        p_w = jnp.exp2(sw - jnp.max(sw, axis=0, keepdims=True)).astype(BF16)
        wt0 = start[u] // tq
        vw = jnp.concatenate([vwt_ref[wt0 + i] for i in range(n_wt)], axis=1)
        acc_w.append(_dot(vw, p_w))

    jrow = lax.broadcasted_iota(jnp.int32, (nj, tq), 0)
    score = []
    for u in subs:
        tok = t0[u] + lax.broadcasted_iota(jnp.int32, (nj, tq), 1)
        cur = tok // SLC_LEN
        forced = (jrow == 0) | (jrow == cur) | (jrow == cur - 1)
        score.append(jnp.where(forced, FORCE_SCORE, jnp.where(jrow * SLC_LEN <= tok, imp[u], NEG_INF)))
    for _ in range(top_n):
        for u in subs:
            mx = jnp.max(score[u], axis=0, keepdims=True)
            first = jnp.min(jnp.where(score[u] == mx, jrow, nj), axis=0, keepdims=True)
            score[u] = jnp.where(jrow == first, REMOVED_SCORE, score[u])
    q_aug = []
    for u in subs:
        bias = jnp.where(score[u] < 0.5 * REMOVED_SCORE, 0.0, NEG_INF).T.astype(BF16)
        q_aug.append(jnp.concatenate([qs[u], jnp.concatenate([bias] * hpg, axis=0)], axis=1))

    def slc_tile(kj, m_run, diagonal):
        k0 = pl.multiple_of(kj * tk, tk)
        ka = kaug_ref[pl.ds(k0, tk), :]
        vt = vst_ref[kj]
        out = []
        scores = [_nt_dot(ka, q_aug[u]) for u in subs]
        for u in subs:
            s = scores[u]
            if diagonal:
                s = jnp.where(k0 + lax.broadcasted_iota(jnp.int32, (tk, 1), 0) <= tcol[u], s, NEG_INF)
            m_new = jnp.maximum(m_run[u], jnp.max(s, axis=0, keepdims=True))
            p = jnp.exp2(s - m_new).astype(BF16)
            acc_ref[u] = jnp.exp2(m_run[u] - m_new) * acc_ref[u] + _dot(vt, p)
            out.append(m_new)
        return tuple(out)

    n_full = (qi * n_sub * tq) // tk
    for u in subs:
        acc_ref[u] = jnp.zeros((LANES + SUBLANES, r), F32)
    m_run = (jnp.full((1, r), NEG_INF, F32),) * n_sub
    m_run = lax.fori_loop(0, n_full, lambda kj, c: slc_tile(kj, c, False), m_run)
    slc_tile(n_full, m_run, True)

    for u in subs:
        acc_s = acc_ref[u]
        g_t = gate_ref[0, u * tq:(u + 1) * tq, :].T

        def gate_row(br):
            return jnp.concatenate([g_t[br * hpg + h:br * hpg + h + 1, :] for h in range(hpg)], axis=1)

        o_t = (gate_row(0) * o_cmp[u]
               + (gate_row(1) * (1.0 / acc_s[LANES:LANES + 1])) * acc_s[0:LANES]
               + (gate_row(2) * (1.0 / acc_w[u][LANES:LANES + 1])) * acc_w[u][0:LANES])
        for h in range(hpg):
            o_ref[0, u * tq:(u + 1) * tq, h * LANES:(h + 1) * LANES] = (
                o_t[:, h * tq:(h + 1) * tq].T.astype(o_ref.dtype))


def _nsa(q, gates, kv, kc, vct, overlap_t, *, n_groups, hpg, top_n, tq, n_sub, tk, name):
    b, s, _ = q.shape
    nc = kc.shape[2]
    nj = overlap_t.shape[0]
    g_ = n_groups
    seg = lambda sidx: (lambda bi, gi, qi: (bi, 0, sidx * g_ + gi))
    bg = lambda bi, gi, qi: (bi * g_ + gi, 0, 0, 0)
    vrows = LANES + SUBLANES
    tqs = n_sub * tq
    assert tk % tqs == 0 and s % tk == 0
    return pl.pallas_call(
        functools.partial(_nsa_kernel, tq=tq, n_sub=n_sub, tk=tk, hpg=hpg, top_n=top_n),
        grid=(b, g_, s // tqs),
        in_specs=[pl.BlockSpec((1, tqs, hpg * LANES), lambda bi, gi, qi: (bi, qi, gi)),
                  pl.BlockSpec((1, tqs, LANES), lambda bi, gi, qi: (bi, qi, gi)),
                  pl.BlockSpec((1, 1, nc, LANES), bg),
                  pl.BlockSpec((1, 1, LANES, nc), bg),
                  pl.BlockSpec((1, s, LANES), seg(KV_K_SLC)),
                  pl.BlockSpec((1, s, LANES), seg(KV_V_SLC)),
                  pl.BlockSpec((1, s, LANES), seg(KV_K_WIN)),
                  pl.BlockSpec((1, s, LANES), seg(KV_V_WIN)),
                  pl.BlockSpec((nj, nc), lambda bi, gi, qi: (0, 0))],
        out_specs=pl.BlockSpec((1, tqs, hpg * LANES), lambda bi, gi, qi: (bi, qi, gi)),
        out_shape=jax.ShapeDtypeStruct(q.shape, BF16),
        scratch_shapes=[pltpu.VMEM((s, LANES + nj), BF16),
                        pltpu.VMEM((s // tk, vrows, tk), BF16),
                        pltpu.VMEM((s // tq, vrows, tq), BF16),
                        pltpu.VMEM((n_sub, vrows, hpg * tq), F32)],
        compiler_params=_cparams(3, 48),
        name=name,
    )(q, gates, kc, vct, kv, kv, kv, kv, overlap_t)


KV_SEGS = (2, 4, 0, 1, 3, 5)
KV_ROPE_SRC = (2, 4)
KV_K_SLC, KV_K_WIN, KV_K_CMP, KV_V_CMP, KV_V_SLC, KV_V_WIN = range(6)

NSA_TQ = 128
NSA_SUB = 4
NSA_TK = 512


def _dilated_layer(h, g_attn, w_in, w_out, tables, batch, seq):
    m, d = h.shape
    ng = len(DILATED_GROUPS)
    dils = tuple(dl for _, dl in DILATED_GROUPS)
    hg = w_out.shape[0] // HEAD_DIM
    wd = hg * HEAD_DIM
    reach = DILATED_GROUPS[0][0] // DILATED_GROUPS[0][1]
    assert all(win // dl == reach for win, dl in DILATED_GROUPS)
    w4 = w_in.reshape(d, 3, ng, wd)
    ws = [jnp.concatenate([_perm_head_cols(w4[:, 0, gi]), _perm_head_cols(w4[:, 1, gi]), w4[:, 2, gi]],
                          axis=1).astype(BF16) for gi in range(ng)]
    tm = min(512, seq)
    os_, lses = [], []
    for gi, dil in enumerate(dils):
        qkv = _a_qkv(h, g_attn, ws[gi], tables, batch=batch, seq=seq, dil=dil,
                     part_width=wd, tm=tm, name=f"a_qkv{gi}")
        o, lse = _dil_attn(qkv, reach=reach, n_heads=hg, t=min(256, seq // dil), name=f"a_attn{gi}")
        os_.append(o)
        lses.append(lse)
    return _mix_out(os_, lses, w_out.astype(BF16), h, seq=seq, n_heads=hg, tm=tm, name="a_out")


def _nsa_shared_kv(h, kv_norm_g, w_kv, cmp_k, cmp_v, tables, batch, seq):
    g_ = NSA_KV_GROUPS
    seg_w = g_ * HEAD_DIM
    (pe_k, w1_k, w2_k), (pe_v, w1_v, w2_v) = cmp_k, cmp_v
    src = lambda s: w_kv[:, s * seg_w:(s + 1) * seg_w]
    w = jnp.concatenate([_perm_head_cols(src(s)) if s in KV_ROPE_SRC else src(s) for s in KV_SEGS],
                        axis=1).astype(BF16)
    kv = _norm_matmul(h, kv_norm_g, w, out_dtype=BF16, tables=tables, seq=seq,
                      tab_fn=lambda j: jnp.where(j == 0, ROPE_K, ROPE_NONE), tn=len(KV_ROPE_SRC) * seg_w,
                      name="b_kv")
    n_chunk = seq // CMP_STRIDE
    n_cmp = (seq - CMP_LEN) // CMP_STRIDE + 1
    assert CMP_LEN == 2 * CMP_STRIDE

    assert KV_V_CMP == KV_K_CMP + 1
    z = kv[:, KV_K_CMP * seg_w:(KV_V_CMP + 1) * seg_w].reshape(batch, n_chunk, CMP_STRIDE, 2, g_, HEAD_DIM)
    z = z.transpose(3, 0, 4, 1, 2, 5).reshape(2, batch * g_, n_chunk, CMP_STRIDE * HEAD_DIM)
    pe = jnp.stack([pe_k.reshape(1, -1), pe_v.reshape(1, -1)])
    pe = jnp.broadcast_to(pe, (2, SUBLANES, pe.shape[-1]))
    cmp_tab = _rope_tables(jnp.arange(n_chunk) * CMP_STRIDE + CMP_LEN - 1, (1.0,))
    cmp_kv = _compress(z, jnp.stack([w1_k, w1_v]).astype(BF16),
                       jnp.stack([_perm_head_cols(w2_k), w2_v]).astype(BF16),
                       pe, cmp_tab, n_cmp=n_cmp, name="b_compress")
    kc = cmp_kv[0][:, None]
    vct = cmp_kv[1].transpose(0, 2, 1)[:, None]
    return kv, kc, vct


def _nsa_layer(h, g_attn, w_qg, w_out, shared, tables, batch, seq):
    m, d = h.shape
    g_ = NSA_KV_GROUPS
    n_heads = w_out.shape[0] // HEAD_DIM
    hpg = n_heads // g_
    seg_w = g_ * HEAD_DIM
    kv, kc, vct = shared
    n_chunk = seq // CMP_STRIDE
    n_cmp = (seq - CMP_LEN) // CMP_STRIDE + 1
    n_q = n_heads * HEAD_DIM
    q = _norm_matmul(h, g_attn, _perm_head_cols(w_qg[:, :n_q]).astype(BF16), out_dtype=BF16,
                     tables=tables, seq=seq, tab_fn=lambda j: ROPE_Q, tn=1024,
                     name="b_q")
    wg = w_qg[:, n_q:].reshape(d, N_NSA_BRANCH, g_, hpg).transpose(0, 2, 1, 3).reshape(d, g_, N_NSA_BRANCH * hpg)
    wg = jnp.pad(wg, ((0, 0), (0, 0), (0, LANES - N_NSA_BRANCH * hpg))).reshape(d, g_ * LANES)
    gates = _norm_matmul(h, g_attn, wg.astype(BF16), out_dtype=F32, act="sigmoid", tn=g_ * LANES,
                         name="b_gates")

    n_slc = seq // SLC_LEN
    nj = -(-n_slc // LANES) * LANES
    cs = np.arange(n_chunk)[:, None] * CMP_STRIDE
    ss = np.arange(nj)[None, :] * SLC_LEN
    ov = np.clip(np.minimum(cs + CMP_LEN, ss + SLC_LEN) - np.maximum(cs, ss), 0, None) / CMP_LEN
    ov[n_cmp:, :] = 0.0
    overlap_t = jnp.asarray(ov.T, BF16)

    o = _nsa(q.reshape(batch, seq, n_q), gates.reshape(batch, seq, g_ * LANES),
             kv.reshape(batch, seq, 6 * seg_w), kc, vct, overlap_t,
             n_groups=g_, hpg=hpg, top_n=min(SLC_TOP_N, n_slc), tq=NSA_TQ, n_sub=NSA_SUB, tk=min(NSA_TK, seq),
             name="b_nsa")
    return _matmul_resid(o.reshape(m, n_q), w_out.astype(BF16), h, name="b_out")


def kernel(x, p, a_w_in, a_w_out, b_w_qg, b_w_out, kv_norm_g, w_kv_shared, cmp_pe_k, cmp_w1_k, cmp_w2_k,
           cmp_pe_v, cmp_w1_v, cmp_w2_v, attn_norm_g, mlp_norm_g, mlp_w1, mlp_w2, ple_norm_g, ple_w_gate,
           ple_w_proj, final_norm_g):
    batch, seq, d = x.shape
    m = batch * seq
    depth = attn_norm_g.shape[0]
    n_a = a_w_in.shape[0]
    h = x.reshape(m, d)
    tables = _seq_rope_tables(seq)
    shared = None
    for i in range(depth):
        if i < n_a:
            h = _dilated_layer(h, attn_norm_g[i], a_w_in[i], a_w_out[i], tables, batch, seq)
        else:
            if i == n_a:
                shared = _nsa_shared_kv(h, kv_norm_g, w_kv_shared, (cmp_pe_k, cmp_w1_k, cmp_w2_k),
                                        (cmp_pe_v, cmp_w1_v, cmp_w2_v), tables, batch, seq)
            j = i - n_a
            h = _nsa_layer(h, attn_norm_g[i], b_w_qg[j], b_w_out[j], shared, tables, batch, seq)
        h = _mlp(h, mlp_norm_g[i], mlp_w1[i].astype(BF16), mlp_w2[i].astype(BF16), name=f"mlp{i}")
        h = _ple(h, ple_norm_g[i], ple_w_gate[i].astype(BF16), p[i].reshape(m, -1),
                 ple_w_proj[i].astype(BF16), final_g=final_norm_g if i == depth - 1 else None, name=f"ple{i}")
    if depth == 0:
        h = _rmsnorm(h, final_norm_g, name="final_norm")
    return h.reshape(batch, seq, d)
```

```python
import functools
import math

import jax
import jax.numpy as jnp
import numpy as np
from jax import lax
from jax.experimental import pallas as pl
from jax.experimental.pallas import tpu as pltpu

F32 = jnp.float32
BF16 = jnp.bfloat16

HEAD_DIM = 128
ROT_DIM = HEAD_DIM // 4
ROPE_THETA = 500000.0
NORM_EPS = 1e-6
NEG_INF = -1e30
FORCE_SCORE = 1e9
TINY = 1e-20
REMOVED_SCORE = -3.0e38
LOG2E = math.log2(math.e)

DILATED_GROUPS = ((128, 1), (512, 4), (2048, 16))
NSA_KV_GROUPS = 4
CMP_LEN = 32
CMP_STRIDE = 16
SLC_LEN = 64
SLC_TOP_N = 16
WIN_LEN = 512
N_NSA_BRANCH = 3

LANES = 128
SUBLANES = 8
MIB = 1024 * 1024


def _cparams(n_grid, vmem_mib):
    return pltpu.CompilerParams(dimension_semantics=("arbitrary",) * n_grid,
                                vmem_limit_bytes=vmem_mib * MIB)


def _nt_dot(a, b):
    return lax.dot_general(a, b, (((1,), (1,)), ((), ())), preferred_element_type=F32)


def _dot(a, b):
    return jnp.dot(a, b, preferred_element_type=F32)


def _perm_head_cols(w):
    k, n = w.shape
    half = ROT_DIM // 2
    w = w.reshape(k, n // HEAD_DIM, HEAD_DIM)
    w = jnp.concatenate([w[..., :half], w[..., ROT_DIM:LANES // 2 + half], w[..., half:ROT_DIM],
                         w[..., LANES // 2 + half:]], axis=-1)
    return w.reshape(k, n)


def _rope_tables(pos, scales):
    n = pos.shape[0]
    half = ROT_DIM // 2
    inv = ROPE_THETA ** (-jnp.arange(half, dtype=F32) * (2.0 / ROT_DIM))
    ang = pos.astype(F32)[:, None] * inv[None, :]
    c, s = jnp.cos(ang), jnp.sin(ang)
    ones = jnp.ones((n, LANES // 2 - half), F32)
    zeros = jnp.zeros((n, LANES // 2 - half), F32)
    rot = jnp.stack([jnp.concatenate([c, ones, c, ones], axis=1),
                     jnp.concatenate([-s, zeros, s, zeros], axis=1)])
    ident = jnp.stack([jnp.ones((n, LANES), F32), jnp.zeros((n, LANES), F32)])
    return jnp.stack([rot * sc for sc in scales] + [ident])


ROPE_Q, ROPE_K, ROPE_NONE = range(3)


def _seq_rope_tables(seq):
    return _rope_tables(jnp.arange(seq), (HEAD_DIM ** -0.5 * LOG2E, 1.0))


def _apply_rope(a, cos, sg):
    return a * cos + pltpu.roll(a, LANES // 2, 1) * sg


def _rms_scale(x, g):
    ms = jnp.mean(x * x, axis=-1, keepdims=True)
    return (x * lax.rsqrt(ms + NORM_EPS) * g).astype(BF16)


def _norm_matmul_kernel(x_ref, g_ref, w_ref, *rest, rope, act):
    if rope:
        tab_ref, o_ref, xn_ref = rest
    else:
        o_ref, xn_ref = rest

    @pl.when(pl.program_id(1) == 0)
    def _():
        xn_ref[...] = _rms_scale(x_ref[...], g_ref[...])

    acc = _dot(xn_ref[...], w_ref[...])
    if rope:
        cos, sg = tab_ref[0, 0], tab_ref[0, 1]
        for c in range(acc.shape[1] // LANES):
            sl = slice(c * LANES, (c + 1) * LANES)
            o_ref[:, sl] = _apply_rope(acc[:, sl], cos, sg).astype(o_ref.dtype)
    else:
        if act == "sigmoid":
            acc = 1.0 / (1.0 + jnp.exp(-acc))
        o_ref[...] = acc.astype(o_ref.dtype)


def _norm_matmul(x, g, w, *, out_dtype, tables=None, tab_fn=None, seq=None, act=None,
                 tm=512, tn=512, name):
    m, k = x.shape
    n = w.shape[1]
    tm, tn = min(tm, m), min(tn, n)
    in_specs = [pl.BlockSpec((tm, k), lambda i, j: (i, 0)),
                pl.BlockSpec((1, k), lambda i, j: (0, 0)),
                pl.BlockSpec((k, tn), lambda i, j: (0, j))]
    args = [x, g.reshape(1, k), w]
    if tables is not None:
        nsb = seq // tm
        in_specs.append(pl.BlockSpec((1, 2, tm, LANES), lambda i, j: (tab_fn(j), 0, i % nsb, 0)))
        args.append(tables)
    return pl.pallas_call(
        functools.partial(_norm_matmul_kernel, rope=tables is not None, act=act),
        grid=(m // tm, n // tn),
        in_specs=in_specs,
        out_specs=pl.BlockSpec((tm, tn), lambda i, j: (i, j)),
        out_shape=jax.ShapeDtypeStruct((m, n), out_dtype),
        scratch_shapes=[pltpu.VMEM((tm, k), BF16)],
        compiler_params=_cparams(2, 40),
        name=name,
    )(*args)


def _matmul_resid_kernel(a_ref, w_ref, r_ref, o_ref):
    o_ref[...] = r_ref[...] + _dot(a_ref[...], w_ref[...])


def _matmul_resid(a, w, r, *, tm=512, tn=1024, name):
    m, k = a.shape
    n = w.shape[1]
    tm, tn = min(tm, m), min(tn, n)
    return pl.pallas_call(
        _matmul_resid_kernel,
        grid=(m // tm, n // tn),
        in_specs=[pl.BlockSpec((tm, k), lambda i, j: (i, 0)),
                  pl.BlockSpec((k, tn), lambda i, j: (0, j)),
                  pl.BlockSpec((tm, tn), lambda i, j: (i, j))],
        out_specs=pl.BlockSpec((tm, tn), lambda i, j: (i, j)),
        out_shape=jax.ShapeDtypeStruct((m, n), F32),
        compiler_params=_cparams(2, 40),
        name=name,
    )(a, w, r)


def _mlp_kernel(x_ref, g_ref, w1_ref, w2_ref, o_ref, xn_ref, acc_ref):
    f = pl.program_id(1)

    @pl.when(f == 0)
    def _():
        xn_ref[...] = _rms_scale(x_ref[...], g_ref[...])
        acc_ref[...] = jnp.zeros_like(acc_ref)

    a = _dot(xn_ref[...], w1_ref[...])
    a = jnp.square(jnp.maximum(a, 0.0)).astype(BF16)
    acc_ref[...] += _dot(a, w2_ref[...])

    @pl.when(f == pl.num_programs(1) - 1)
    def _():
        o_ref[...] = x_ref[...] + acc_ref[...]


def _mlp(h, g, w1, w2, *, tm=512, tf=1024, name):
    m, d = h.shape
    ff = w1.shape[1]
    tm, tf = min(tm, m), min(tf, ff)
    return pl.pallas_call(
        _mlp_kernel,
        grid=(m // tm, ff // tf),
        in_specs=[pl.BlockSpec((tm, d), lambda i, f: (i, 0)),
                  pl.BlockSpec((1, d), lambda i, f: (0, 0)),
                  pl.BlockSpec((d, tf), lambda i, f: (0, f)),
                  pl.BlockSpec((tf, d), lambda i, f: (f, 0))],
        out_specs=pl.BlockSpec((tm, d), lambda i, f: (i, 0)),
        out_shape=jax.ShapeDtypeStruct((m, d), F32),
        scratch_shapes=[pltpu.VMEM((tm, d), BF16), pltpu.VMEM((tm, d), F32)],
        compiler_params=_cparams(2, 56),
        name=name,
    )(h, g.reshape(1, d), w1, w2)


PLE_COL_CHUNK = 1024


def _ple_kernel(x_ref, g_ref, wg_ref, p_ref, wp_ref, *rest, final):
    if final:
        fg_ref, o_ref = rest
    else:
        (o_ref,) = rest
    n = o_ref.shape[1]
    xn = _rms_scale(x_ref[...], g_ref[...])
    pb = p_ref[...].astype(BF16)
    ssq = None
    for c0 in range(0, n, PLE_COL_CHUNK):
        sl = slice(c0, c0 + PLE_COL_CHUNK)
        gate = 1.0 / (1.0 + jnp.exp(-_dot(xn, wg_ref[:, sl])))
        hc = x_ref[:, sl] + _dot(pb, wp_ref[:, sl]) * gate
        o_ref[:, sl] = hc
        if final:
            part = jnp.sum(hc * hc, axis=-1, keepdims=True)
            ssq = part if ssq is None else ssq + part
    if final:
        o_ref[...] = o_ref[...] * lax.rsqrt(ssq * (1.0 / n) + NORM_EPS) * fg_ref[...]


def _ple(h, g, wg, p, wp, *, final_g=None, tm=512, name):
    m, d = h.shape
    dp = p.shape[1]
    n = wg.shape[1]
    assert n == d and n % PLE_COL_CHUNK == 0
    tm = min(tm, m)
    row = lambda i: (i, 0)
    const = lambda i: (0, 0)
    in_specs = [pl.BlockSpec((tm, d), row), pl.BlockSpec((1, d), const), pl.BlockSpec((d, n), const),
                pl.BlockSpec((tm, dp), row), pl.BlockSpec((dp, n), const)]
    args = [h, g.reshape(1, d), wg, p, wp]
    if final_g is not None:
        in_specs.append(pl.BlockSpec((1, n), const))
        args.append(final_g.reshape(1, n))
    return pl.pallas_call(
        functools.partial(_ple_kernel, final=final_g is not None),
        grid=(m // tm,),
        in_specs=in_specs,
        out_specs=pl.BlockSpec((tm, n), row),
        out_shape=jax.ShapeDtypeStruct((m, n), F32),
        compiler_params=_cparams(1, 56),
        name=name,
    )(*args)


def _rmsnorm_kernel(x_ref, g_ref, o_ref):
    x = x_ref[...]
    ms = jnp.mean(x * x, axis=-1, keepdims=True)
    o_ref[...] = x * lax.rsqrt(ms + NORM_EPS) * g_ref[...]


def _rmsnorm(h, g, *, tm=512, name):
    m, d = h.shape
    tm = min(tm, m)
    return pl.pallas_call(
        _rmsnorm_kernel,
        grid=(m // tm,),
        in_specs=[pl.BlockSpec((tm, d), lambda i: (i, 0)),
                  pl.BlockSpec((1, d), lambda i: (0, 0))],
        out_specs=pl.BlockSpec((tm, d), lambda i: (i, 0)),
        out_shape=jax.ShapeDtypeStruct((m, d), F32),
        compiler_params=_cparams(1, 32),
        name=name,
    )(h, g.reshape(1, d))


def _a_qkv_kernel(x_ref, g_ref, w_ref, tab_ref, o_ref, xn_ref, res_ref, *, dil):
    @pl.when(pl.program_id(1) == 0)
    def _():
        xn_ref[...] = _rms_scale(x_ref[...], g_ref[...])

    acc = _dot(xn_ref[...], w_ref[...])
    tm, tn = acc.shape
    cos, sg = tab_ref[0, 0], tab_ref[0, 1]
    for c in range(tn // LANES):
        sl = slice(c * LANES, (c + 1) * LANES)
        res = _apply_rope(acc[:, sl], cos, sg)
        if dil == 1:
            o_ref[0, 0, :, sl] = res.astype(BF16)
        else:
            res_ref[c] = res
            for r in range(dil):
                o_ref[0, r, :, sl] = res_ref[c, pl.ds(r, tm // dil, stride=dil), :].astype(BF16)


def _a_qkv(x, g, w, tables, *, batch, seq, dil, part_width, tm=512, tn=1024, name):
    m, k = x.shape
    n = w.shape[1]
    nsb = seq // tm
    bpp = part_width // tn
    assert tm % (dil * 16) == 0
    return pl.pallas_call(
        functools.partial(_a_qkv_kernel, dil=dil),
        grid=(m // tm, n // tn),
        in_specs=[pl.BlockSpec((tm, k), lambda i, j: (i, 0)),
                  pl.BlockSpec((1, k), lambda i, j: (0, 0)),
                  pl.BlockSpec((k, tn), lambda i, j: (0, j)),
                  pl.BlockSpec((1, 2, tm, LANES), lambda i, j: (j // bpp, 0, i % nsb, 0))],
        out_specs=pl.BlockSpec((1, dil, tm // dil, tn), lambda i, j: (i // nsb, 0, i % nsb, j)),
        out_shape=jax.ShapeDtypeStruct((batch, dil, seq // dil, n), BF16),
        scratch_shapes=[pltpu.VMEM((tm, k), BF16), pltpu.VMEM((tn // LANES, tm, LANES), F32)],
        compiler_params=_cparams(2, 40),
        name=name,
    )(x, g.reshape(1, k), w, tables)


def _dil_attn_kernel(q_ref, kp_ref, kc_ref, vp_ref, vc_ref, o_ref, lse_ref, *, reach, n_heads):
    i = pl.program_id(2)
    t = q_ref.shape[2]
    tp = kp_ref.shape[2]
    row = lax.broadcasted_iota(jnp.int32, (t, tp + t), 0)
    col = lax.broadcasted_iota(jnp.int32, (t, tp + t), 1)
    no_prev = jnp.where(i > 0, 0, reach + 1 + tp + t)
    dist = row + tp - col + jnp.where(col < tp, no_prev, 0)
    valid = dist.astype(jnp.uint32) <= reach
    lane = lax.broadcasted_iota(jnp.int32, (t, LANES), 1)
    lse_tile = jnp.zeros((t, LANES), F32)
    heads = [slice(h * LANES, (h + 1) * LANES) for h in range(n_heads)]
    scores = [_nt_dot(q_ref[0, 0, :, sl], jnp.concatenate([kp_ref[0, 0, :, sl], kc_ref[0, 0, :, sl]], axis=0))
              for sl in heads]
    for h, sl in enumerate(heads):
        v = jnp.concatenate([vp_ref[0, 0, :, sl], vc_ref[0, 0, :, sl]], axis=0)
        s = jnp.where(valid, scores[h], NEG_INF)
        m = jnp.max(s, axis=-1, keepdims=True)
        p = jnp.exp2(s - m)
        l = jnp.sum(p, axis=-1, keepdims=True)
        o_ref[0, 0, :, sl] = (_dot(p.astype(BF16), v) * (1.0 / l)).astype(o_ref.dtype)
        lse_tile = jnp.where(lane == h, m + jnp.log2(l), lse_tile)
    lse_ref[0, 0] = lse_tile


def _dil_attn(qkv, *, reach, n_heads, t, name):
    batch, dil, su, _ = qkv.shape
    w = n_heads * LANES
    tp = reach
    assert t % tp == 0 and tp % 16 == 0
    cur = lambda part: (lambda b, r, i: (b, r, i, part))
    prev = lambda part: (lambda b, r, i: (b, r, jnp.maximum(i * (t // tp) - 1, 0), part))
    return pl.pallas_call(
        functools.partial(_dil_attn_kernel, reach=reach, n_heads=n_heads),
        grid=(batch, dil, su // t),
        in_specs=[pl.BlockSpec((1, 1, t, w), cur(0)),
                  pl.BlockSpec((1, 1, tp, w), prev(1)),
                  pl.BlockSpec((1, 1, t, w), cur(1)),
                  pl.BlockSpec((1, 1, tp, w), prev(2)),
                  pl.BlockSpec((1, 1, t, w), cur(2))],
        out_specs=[pl.BlockSpec((1, 1, t, w), lambda b, r, i: (b, r, i, 0)),
                   pl.BlockSpec((1, 1, t, LANES), lambda b, r, i: (b, r, i, 0))],
        out_shape=[jax.ShapeDtypeStruct((batch, dil, su, w), BF16),
                   jax.ShapeDtypeStruct((batch, dil, su, LANES), F32)],
        compiler_params=_cparams(3, 40),
        name=name,
    )(qkv, qkv, qkv, qkv, qkv)


def _mix_out_kernel(*refs, dils, n_heads):
    n_g = len(dils)
    o_refs = refs[:n_g]
    lse_refs = refs[n_g:2 * n_g]
    w_ref, r_ref, out_ref, oc_ref, uo_ref, ul_ref = refs[2 * n_g:]
    tm = oc_ref.shape[0]
    for g, d in enumerate(dils):
        for r in range(d):
            rows = pl.ds(r, tm // d, stride=d)
            for h in range(n_heads):
                uo_ref[g, h, rows, :] = o_refs[g][0, r, :, h * LANES:(h + 1) * LANES].astype(F32)
            ul_ref[g, rows, :] = lse_refs[g][0, r]
    m = ul_ref[0]
    for g in range(1, n_g):
        m = jnp.maximum(m, ul_ref[g])
    e = [jnp.exp2(ul_ref[g] - m) for g in range(n_g)]
    den = e[0]
    for g in range(1, n_g):
        den = den + e[g]
    inv = 1.0 / den
    for h in range(n_heads):
        sl = slice(h * LANES, (h + 1) * LANES)
        acc = (e[0] * inv)[:, h:h + 1] * uo_ref[0, h]
        for g in range(1, n_g):
            acc = acc + (e[g] * inv)[:, h:h + 1] * uo_ref[g, h]
        oc_ref[:, sl] = acc.astype(BF16)
    out_ref[...] = r_ref[...] + _dot(oc_ref[...], w_ref[...])


def _mix_out(os_, lses, w, r, *, seq, n_heads, tm=512, name):
    m, n = r.shape
    k = w.shape[0]
    dils = tuple(o.shape[1] for o in os_)
    nsb = seq // tm
    cls = lambda i: (i // nsb, 0, i % nsb, 0)
    return pl.pallas_call(
        functools.partial(_mix_out_kernel, dils=dils, n_heads=n_heads),
        grid=(m // tm,),
        in_specs=([pl.BlockSpec((1, d, tm // d, k), cls) for d in dils]
                  + [pl.BlockSpec((1, d, tm // d, LANES), cls) for d in dils]
                  + [pl.BlockSpec((k, n), lambda i: (0, 0)),
                     pl.BlockSpec((tm, n), lambda i: (i, 0))]),
        out_specs=pl.BlockSpec((tm, n), lambda i: (i, 0)),
        out_shape=jax.ShapeDtypeStruct((m, n), F32),
        scratch_shapes=[pltpu.VMEM((tm, k), BF16), pltpu.VMEM((len(dils), n_heads, tm, LANES), F32),
                        pltpu.VMEM((len(dils), tm, LANES), F32)],
        compiler_params=_cparams(1, 56),
        name=name,
    )(*os_, *lses, w, r)


def _cmp_kernel(z_ref, w1_ref, w2_ref, pe_ref, tab_ref, o_ref, *, n_cmp):
    w1 = w1_ref[0]
    z = z_ref[0, 0]
    nc, half = z.shape
    bias = _dot(pe_ref[0].astype(BF16), w1)[0:1]
    hid = _dot(z, w1[:half]) + pltpu.roll(_dot(z, w1[half:]), nc - 1, 0) + bias
    hid = 0.5 * hid * (1.0 + jnp.tanh(math.sqrt(2.0 / math.pi) * (hid + 0.044715 * (hid * hid * hid))))
    o = _apply_rope(_dot(hid.astype(BF16), w2_ref[0]), tab_ref[0, 0], tab_ref[0, 1])
    row = lax.broadcasted_iota(jnp.int32, o.shape, 0)
    o_ref[0, 0] = jnp.where(row < n_cmp, o, 0.0).astype(o_ref.dtype)


def _compress(z, w1, w2, pe, tab, *, n_cmp, name):
    two, bg, nc, half = z.shape
    ld = w1.shape[1]
    hid = w1.shape[2]
    return pl.pallas_call(
        functools.partial(_cmp_kernel, n_cmp=n_cmp),
        grid=(two, bg),
        in_specs=[pl.BlockSpec((1, 1, nc, half), lambda s, b: (s, b, 0, 0)),
                  pl.BlockSpec((1, ld, hid), lambda s, b: (s, 0, 0)),
                  pl.BlockSpec((1, hid, LANES), lambda s, b: (s, 0, 0)),
                  pl.BlockSpec((1, SUBLANES, ld), lambda s, b: (s, 0, 0)),
                  pl.BlockSpec((1, 2, nc, LANES), lambda s, b: (s, 0, 0, 0))],
        out_specs=pl.BlockSpec((1, 1, nc, LANES), lambda s, b: (s, b, 0, 0)),
        out_shape=jax.ShapeDtypeStruct((two, bg, nc, LANES), BF16),
        compiler_params=_cparams(2, 40),
        name=name,
    )(z, w1, w2, pe, tab)


def _nsa_kernel(q_ref, gate_ref, kc_ref, vct_ref, ks_ref, vs_ref, kw_ref, vw_ref, ovt_ref, o_ref, kaug_ref,
                vst_ref, vwt_ref, acc_ref, *, tq, n_sub, tk, hpg, top_n):
    qi = pl.program_id(2)
    r = hpg * tq
    seq = ks_ref.shape[1]
    nj = ovt_ref.shape[0]
    nc = kc_ref.shape[2]

    @pl.when(qi == 0)
    def _():
        def fill(c, carry):
            r0 = pl.multiple_of(c * tk, tk)
            kaug_ref[pl.ds(r0, tk), 0:LANES] = ks_ref[0, pl.ds(r0, tk), :]
            blk = (r0 + lax.broadcasted_iota(jnp.int32, (tk, nj), 0)) // SLC_LEN
            hot = jnp.where(blk == lax.broadcasted_iota(jnp.int32, (tk, nj), 1), 1.0, 0.0)
            kaug_ref[pl.ds(r0, tk), LANES:LANES + nj] = hot.astype(BF16)
            per = tk // tq
            for src_ref, dst_ref, width, base in ((vs_ref, vst_ref, tk, c), (vw_ref, vwt_ref, tq, c * per)):
                vt = src_ref[0, pl.ds(r0, tk), :].astype(F32).T.astype(BF16)
                for i in range(tk // width):
                    dst_ref[base + i, 0:LANES, :] = vt[:, i * width:(i + 1) * width]
                    dst_ref[base + i, LANES:LANES + SUBLANES, :] = jnp.ones((SUBLANES, width), BF16)
            return carry
        lax.fori_loop(0, seq // tk, fill, 0)

    subs = range(n_sub)
    n_wt = WIN_LEN // tq + 1
    span = n_wt * tq
    t0 = [(qi * n_sub + u) * tq for u in subs]
    qs, tcol = [], []
    for u in subs:
        q = q_ref[u * tq:(u + 1) * tq, :]
        qs.append(jnp.concatenate([q[:, h * LANES:(h + 1) * LANES] for h in range(hpg)], axis=0))
        tcol.append(t0[u] + (lax.broadcasted_iota(jnp.int32, (1, r), 1) & (tq - 1)))

    s_c = [_nt_dot(kc_ref[0, 0], qs[u]) for u in subs]
    start = [pl.multiple_of(jnp.maximum(t0[u] - WIN_LEN, 0), tq) for u in subs]
    s_w = [_nt_dot(kw_ref[0, pl.ds(start[u], span), :], qs[u]) for u in subs]

    cend = lax.broadcasted_iota(jnp.int32, (nc, 1), 0) * CMP_STRIDE + (CMP_LEN - 1)
    ovt = ovt_ref[...]
    o_cmp, imp = [], []
    for u in subs:
        sc = jnp.where(cend <= tcol[u], s_c[u], NEG_INF)
        m_c = jnp.max(sc, axis=0, keepdims=True)
        e_c = jnp.exp2(sc - m_c)
        inv_c = jnp.where(m_c > 0.5 * NEG_INF, 1.0 / jnp.maximum(jnp.sum(e_c, axis=0, keepdims=True), TINY), 0.0)
        p_c = e_c * inv_c
        o_cmp.append(_dot(vct_ref[0, 0], p_c.astype(BF16)))
        psum = p_c[:, 0:tq]
        for h in range(1, hpg):
            psum = psum + p_c[:, h * tq:(h + 1) * tq]
        p_hi = psum.astype(BF16)
        rem = psum - p_hi.astype(F32)
        p_mid = rem.astype(BF16)
        p_lo = (rem - p_mid.astype(F32)).astype(BF16)
        imp.append(_dot(ovt, p_hi) + _dot(ovt, p_mid) + _dot(ovt, p_lo))

    acc_w = []
    for u in subs:
        dist = tcol[u] - (start[u] + lax.broadcasted_iota(jnp.int32, (span, 1), 0))
        sw = jnp.where(dist.astype(jnp.uint32) < WIN_LEN, s_w[u], NEG_INF)
        p_w = jnp.exp2(sw - jnp.max(sw, axis=0, keepdims=True)).astype(BF16)
        wt0 = start[u] // tq
        vw = jnp.concatenate([vwt_ref[wt0 + i] for i in range(n_wt)], axis=1)
        acc_w.append(_dot(vw, p_w))

    jrow = lax.broadcasted_iota(jnp.int32, (nj, tq), 0)
    score = []
    for u in subs:
        tok = t0[u] + lax.broadcasted_iota(jnp.int32, (nj, tq), 1)
        cur = tok // SLC_LEN
        forced = (jrow == 0) | (jrow == cur) | (jrow == cur - 1)
        score.append(jnp.where(forced, FORCE_SCORE, jnp.where(jrow * SLC_LEN <= tok, imp[u], NEG_INF)))
    for _ in range(top_n):
        for u in subs:
            mx = jnp.max(score[u], axis=0, keepdims=True)
            first = jnp.min(jnp.where(score[u] == mx, jrow, nj), axis=0, keepdims=True)
            score[u] = jnp.where(jrow == first, REMOVED_SCORE, score[u])
    q_aug = []
    for u in subs:
        bias = jnp.where(score[u] < 0.5 * REMOVED_SCORE, 0.0, NEG_INF).T.astype(BF16)
        q_aug.append(jnp.concatenate([qs[u], jnp.concatenate([bias] * hpg, axis=0)], axis=1))

    def slc_tile(kj, m_run, diagonal):
        k0 = pl.multiple_of(kj * tk, tk)
        ka = kaug_ref[pl.ds(k0, tk), :]
        vt = vst_ref[kj]
        out = []
        scores = [_nt_dot(ka, q_aug[u]) for u in subs]
        for u in subs:
            s = scores[u]
            if diagonal:
                s = jnp.where(k0 + lax.broadcasted_iota(jnp.int32, (tk, 1), 0) <= tcol[u], s, NEG_INF)
            m_new = jnp.maximum(m_run[u], jnp.max(s, axis=0, keepdims=True))
            p = jnp.exp2(s - m_new).astype(BF16)
            acc_ref[u] = jnp.exp2(m_run[u] - m_new) * acc_ref[u] + _dot(vt, p)
            out.append(m_new)
        return tuple(out)

    n_full = (qi * n_sub * tq) // tk
    for u in subs:
        acc_ref[u] = jnp.zeros((LANES + SUBLANES, r), F32)
    m_run = (jnp.full((1, r), NEG_INF, F32),) * n_sub
    m_run = lax.fori_loop(0, n_full, lambda kj, c: slc_tile(kj, c, False), m_run)
    slc_tile(n_full, m_run, True)

    for u in subs:
        acc_s = acc_ref[u]
        g_t = gate_ref[0, u * tq:(u + 1) * tq, :].T

        def gate_row(br):
            return jnp.concatenate([g_t[br * hpg + h:br * hpg + h + 1, :] for h in range(hpg)], axis=1)

        o_t = (gate_row(0) * o_cmp[u]
               + (gate_row(1) * (1.0 / acc_s[LANES:LANES + 1])) * acc_s[0:LANES]
               + (gate_row(2) * (1.0 / acc_w[u][LANES:LANES + 1])) * acc_w[u][0:LANES])
        for h in range(hpg):
            o_ref[u * tq:(u + 1) * tq, h * LANES:(h + 1) * LANES] = (
                o_t[:, h * tq:(h + 1) * tq].T.astype(o_ref.dtype))


def _nsa(q, gates, kv, kc, vct, overlap_t, *, n_groups, hpg, top_n, tq, n_sub, tk, name):
    b, s, _ = kv.shape
    nc = kc.shape[2]
    nj = overlap_t.shape[0]
    g_ = n_groups
    seg = lambda sidx: (lambda bi, gi, qi: (bi, 0, sidx * g_ + gi))
    bg = lambda bi, gi, qi: (bi * g_ + gi, 0, 0, 0)
    vrows = LANES + SUBLANES
    tqs = n_sub * tq
    assert tk % tqs == 0 and s % tk == 0
    return pl.pallas_call(
        functools.partial(_nsa_kernel, tq=tq, n_sub=n_sub, tk=tk, hpg=hpg, top_n=top_n),
        grid=(b, g_, s // tqs),
        in_specs=[pl.BlockSpec((tqs, hpg * LANES), lambda bi, gi, qi: (bi * (s // tqs) + qi, gi)),
                  pl.BlockSpec((1, tqs, LANES), lambda bi, gi, qi: (bi, qi, gi)),
                  pl.BlockSpec((1, 1, nc, LANES), bg),
                  pl.BlockSpec((1, 1, LANES, nc), bg),
                  pl.BlockSpec((1, s, LANES), seg(KV_K_SLC)),
                  pl.BlockSpec((1, s, LANES), seg(KV_V_SLC)),
                  pl.BlockSpec((1, s, LANES), seg(KV_K_WIN)),
                  pl.BlockSpec((1, s, LANES), seg(KV_V_WIN)),
                  pl.BlockSpec((nj, nc), lambda bi, gi, qi: (0, 0))],
        out_specs=pl.BlockSpec((tqs, hpg * LANES), lambda bi, gi, qi: (bi * (s // tqs) + qi, gi)),
        out_shape=jax.ShapeDtypeStruct(q.shape, BF16),
        scratch_shapes=[pltpu.VMEM((s, LANES + nj), BF16),
                        pltpu.VMEM((s // tk, vrows, tk), BF16),
                        pltpu.VMEM((s // tq, vrows, tq), BF16),
                        pltpu.VMEM((n_sub, vrows, hpg * tq), F32)],
        compiler_params=_cparams(3, 48),
        name=name,
    )(q, gates, kc, vct, kv, kv, kv, kv, overlap_t)


KV_SEGS = (2, 4, 0, 1, 3, 5)
KV_ROPE_SRC = (2, 4)
KV_K_SLC, KV_K_WIN, KV_K_CMP, KV_V_CMP, KV_V_SLC, KV_V_WIN = range(6)

NSA_TQ = 128
NSA_SUB = 4
NSA_TK = 512


def _dilated_layer(h, g_attn, w_in, w_out, tables, batch, seq):
    m, d = h.shape
    ng = len(DILATED_GROUPS)
    dils = tuple(dl for _, dl in DILATED_GROUPS)
    hg = w_out.shape[0] // HEAD_DIM
    wd = hg * HEAD_DIM
    reach = DILATED_GROUPS[0][0] // DILATED_GROUPS[0][1]
    assert all(win // dl == reach for win, dl in DILATED_GROUPS)
    src = lambda part, gi: w_in[:, (part * ng + gi) * wd:(part * ng + gi + 1) * wd]
    ws = [jnp.concatenate([_perm_head_cols(src(0, gi)), _perm_head_cols(src(1, gi)), src(2, gi)],
                          axis=1).astype(BF16) for gi in range(ng)]
    tm = min(512, seq)
    os_, lses = [], []
    for gi, dil in enumerate(dils):
        qkv = _a_qkv(h, g_attn, ws[gi], tables, batch=batch, seq=seq, dil=dil,
                     part_width=wd, tm=tm, name=f"a_qkv{gi}")
        o, lse = _dil_attn(qkv, reach=reach, n_heads=hg, t=min(256, seq // dil), name=f"a_attn{gi}")
        os_.append(o)
        lses.append(lse)
    return _mix_out(os_, lses, w_out.astype(BF16), h, seq=seq, n_heads=hg, tm=tm, name="a_out")


def _nsa_shared_kv(h, kv_norm_g, w_kv, cmp_k, cmp_v, tables, batch, seq):
    g_ = NSA_KV_GROUPS
    seg_w = g_ * HEAD_DIM
    (pe_k, w1_k, w2_k), (pe_v, w1_v, w2_v) = cmp_k, cmp_v
    src = lambda s: w_kv[:, s * seg_w:(s + 1) * seg_w]
    w = jnp.concatenate([_perm_head_cols(src(s)) if s in KV_ROPE_SRC else src(s) for s in KV_SEGS],
                        axis=1).astype(BF16)
    kv = _norm_matmul(h, kv_norm_g, w, out_dtype=BF16, tables=tables, seq=seq,
                      tab_fn=lambda j: jnp.where(j == 0, ROPE_K, ROPE_NONE), tn=len(KV_ROPE_SRC) * seg_w,
                      name="b_kv")
    n_chunk = seq // CMP_STRIDE
    n_cmp = (seq - CMP_LEN) // CMP_STRIDE + 1
    assert CMP_LEN == 2 * CMP_STRIDE

    assert KV_V_CMP == KV_K_CMP + 1
    z = kv[:, KV_K_CMP * seg_w:(KV_V_CMP + 1) * seg_w].reshape(batch, n_chunk, CMP_STRIDE, 2, g_, HEAD_DIM)
    z = z.transpose(3, 0, 4, 1, 2, 5).reshape(2, batch * g_, n_chunk, CMP_STRIDE * HEAD_DIM)
    pe = jnp.stack([pe_k.reshape(1, -1), pe_v.reshape(1, -1)])
    pe = jnp.broadcast_to(pe, (2, SUBLANES, pe.shape[-1]))
    cmp_tab = _rope_tables(jnp.arange(n_chunk) * CMP_STRIDE + CMP_LEN - 1, (1.0,))
    cmp_kv = _compress(z, jnp.stack([w1_k, w1_v]).astype(BF16),
                       jnp.stack([_perm_head_cols(w2_k), w2_v]).astype(BF16),
                       pe, cmp_tab, n_cmp=n_cmp, name="b_compress")
    kc = cmp_kv[0][:, None]
    vct = cmp_kv[1].transpose(0, 2, 1)[:, None]
    return kv, kc, vct


def _nsa_layer(h, g_attn, w_qg, w_out, shared, tables, batch, seq):
    m, d = h.shape
    g_ = NSA_KV_GROUPS
    n_heads = w_out.shape[0] // HEAD_DIM
    hpg = n_heads // g_
    seg_w = g_ * HEAD_DIM
    kv, kc, vct = shared
    n_chunk = seq // CMP_STRIDE
    n_cmp = (seq - CMP_LEN) // CMP_STRIDE + 1
    n_q = n_heads * HEAD_DIM
    q = _norm_matmul(h, g_attn, _perm_head_cols(w_qg[:, :n_q]).astype(BF16), out_dtype=BF16,
                     tables=tables, seq=seq, tab_fn=lambda j: ROPE_Q, tn=1024,
                     name="b_q")
    wg = w_qg[:, n_q:].reshape(d, N_NSA_BRANCH, g_, hpg).transpose(0, 2, 1, 3).reshape(d, g_, N_NSA_BRANCH * hpg)
    wg = jnp.pad(wg, ((0, 0), (0, 0), (0, LANES - N_NSA_BRANCH * hpg))).reshape(d, g_ * LANES)
    gates = _norm_matmul(h, g_attn, wg.astype(BF16), out_dtype=F32, act="sigmoid", tn=g_ * LANES,
                         name="b_gates")

    n_slc = seq // SLC_LEN
    nj = -(-n_slc // LANES) * LANES
    cs = np.arange(n_chunk)[:, None] * CMP_STRIDE
    ss = np.arange(nj)[None, :] * SLC_LEN
    ov = np.clip(np.minimum(cs + CMP_LEN, ss + SLC_LEN) - np.maximum(cs, ss), 0, None) / CMP_LEN
    ov[n_cmp:, :] = 0.0
    overlap_t = jnp.asarray(ov.T, BF16)

    o = _nsa(q, gates.reshape(batch, seq, g_ * LANES),
             kv.reshape(batch, seq, 6 * seg_w), kc, vct, overlap_t,
             n_groups=g_, hpg=hpg, top_n=min(SLC_TOP_N, n_slc), tq=NSA_TQ, n_sub=NSA_SUB, tk=min(NSA_TK, seq),
             name="b_nsa")
    return _matmul_resid(o, w_out.astype(BF16), h, name="b_out")


def kernel(x, p, a_w_in, a_w_out, b_w_qg, b_w_out, kv_norm_g, w_kv_shared, cmp_pe_k, cmp_w1_k, cmp_w2_k,
           cmp_pe_v, cmp_w1_v, cmp_w2_v, attn_norm_g, mlp_norm_g, mlp_w1, mlp_w2, ple_norm_g, ple_w_gate,
           ple_w_proj, final_norm_g):
    batch, seq, d = x.shape
    m = batch * seq
    depth = attn_norm_g.shape[0]
    n_a = a_w_in.shape[0]
    h = x.reshape(m, d)
    tables = _seq_rope_tables(seq)
    shared = None
    for i in range(depth):
        if i < n_a:
            h = _dilated_layer(h, attn_norm_g[i], a_w_in[i], a_w_out[i], tables, batch, seq)
        else:
            if i == n_a:
                shared = _nsa_shared_kv(h, kv_norm_g, w_kv_shared, (cmp_pe_k, cmp_w1_k, cmp_w2_k),
                                        (cmp_pe_v, cmp_w1_v, cmp_w2_v), tables, batch, seq)
            j = i - n_a
            h = _nsa_layer(h, attn_norm_g[i], b_w_qg[j], b_w_out[j], shared, tables, batch, seq)
        h = _mlp(h, mlp_norm_g[i], mlp_w1[i].astype(BF16), mlp_w2[i].astype(BF16), name=f"mlp{i}")
        h = _ple(h, ple_norm_g[i], ple_w_gate[i].astype(BF16), p[i].reshape(m, -1),
                 ple_w_proj[i].astype(BF16), final_g=final_norm_g if i == depth - 1 else None, name=f"ple{i}")
    if depth == 0:
        h = _rmsnorm(h, final_norm_g, name="final_norm")
    return h.reshape(batch, seq, d)
```
